```python
import jax
import jax.numpy as jnp
from jax import lax
import numpy as np

D_MODEL = 4096
BATCH = 32
SEQ = 256
DEPTH = 2
DEC_BATCH = 8
DEC_SEQ = 4096
PAST_LEN = 256

GRID_W = 64
HEAD_DIM = 128
BRANCH_W = D_MODEL // 2
N_HEADS = BRANCH_W // HEAD_DIM
MIX_W = 2 * BRANCH_W
N_AB = (DEPTH + 1) // 2
N_CD = DEPTH // 2
WIN_R = 8
WIN_C = 16
SG_CHUNK = 128
HGRN_DK = 128
HGRN_DV = HEAD_DIM
C_K = N_HEADS * HGRN_DK
HGRN_CHUNK = 32
Q_LORA = D_MODEL // 4
KV_LORA = 512
NOPE_D = 128
ROPE_D = 64
V_D = HEAD_DIM
QK_D = NOPE_D + ROPE_D
ROPE_BASE = 10000.0
Q_BLOCK = 128
EPS = 1e-6
NEG_INF = -1e30
AB_SIZES = (BRANCH_W, BRANCH_W, BRANCH_W, BRANCH_W, BRANCH_W, BRANCH_W, BRANCH_W)
CD_SIZES = (C_K, C_K, C_K, N_HEADS * HGRN_DV, N_HEADS * HGRN_DV, Q_LORA, KV_LORA, ROPE_D, N_HEADS * V_D)
AB_IN = sum(AB_SIZES)
CD_IN = sum(CD_SIZES)

kernel_name = 'hybrid_diffusion_na_sgmlp_hgrn2_mla_step'


def split_cols(p, sizes):
    return jnp.split(p, np.cumsum(sizes)[:-1].tolist(), axis=-1)


def rms_norm(x, w):
    xf = x.astype(jnp.float32)
    y = xf * lax.rsqrt(jnp.mean(xf * xf, axis=-1, keepdims=True) + EPS)
    return (y * w.astype(jnp.float32)).astype(x.dtype)


def modulate(x, norm_w, cond, w_ada, b_ada):
    mod = jax.nn.silu(cond) @ w_ada + b_ada
    shift, scale, gate = jnp.split(mod, 3, axis=-1)
    return rms_norm(x, norm_w) * (1 + scale) + shift, gate


def rope_1d(x, pos):
    half = x.shape[-1] // 2
    inv = ROPE_BASE ** (-jnp.arange(half, dtype=jnp.float32) / half)
    ang = pos.astype(jnp.float32)[:, None] * inv[None, :]
    cos = jnp.cos(ang)[None, :, None, :].astype(x.dtype)
    sin = jnp.sin(ang)[None, :, None, :].astype(x.dtype)
    x1, x2 = x[..., :half], x[..., half:]
    return jnp.concatenate([x1 * cos - x2 * sin, x1 * sin + x2 * cos], axis=-1)


def axial_rope(x):
    t = jnp.arange(x.shape[1])
    h = x.shape[-1] // 2
    return jnp.concatenate([rope_1d(x[..., :h], t // GRID_W), rope_1d(x[..., h:], t % GRID_W)], axis=-1)


def block_attend(q, k, v):
    b, lq, h, dq = q.shape
    nb = lq // Q_BLOCK
    scale = dq ** -0.5
    qb = q.reshape(b, nb, Q_BLOCK, h, dq).swapaxes(0, 1)

    def one_block(qi):
        s = jnp.einsum('bqhd,bkhd->bhqk', qi, k, preferred_element_type=jnp.float32) * scale
        p = jax.nn.softmax(s, axis=-1).astype(v.dtype)
        return jnp.einsum('bhqk,bkhd->bqhd', p, v)

    o = lax.map(one_block, qb)
    return o.swapaxes(0, 1).reshape(b, lq, h, v.shape[-1])


def na_latent(q, k, v, k_ctx, v_ctx, rpb):
    b, l, h, d = q.shape
    rows = l // GRID_W
    kr = min(WIN_R, rows)
    qg = q.reshape(b, rows, GRID_W, h, d)
    kg = k.reshape(b, rows, GRID_W, h, d)
    vg = v.reshape(b, rows, GRID_W, h, d)
    cols = jnp.arange(GRID_W)
    cstart = jnp.clip(cols - WIN_C // 2, 0, GRID_W - WIN_C)
    col_mask = (cols[None, :] >= cstart[:, None]) & (cols[None, :] < cstart[:, None] + WIN_C)
    dc_idx = jnp.clip(cols[None, :] - cols[:, None] + WIN_C - 1, 0, 2 * WIN_C - 2)
    rpb32 = rpb.astype(jnp.float32)
    scale = d ** -0.5

    def one_row(r):
        rs = jnp.clip(r - kr // 2, 0, rows - kr)
        kb = lax.dynamic_slice_in_dim(kg, rs, kr, axis=1)
        vb = lax.dynamic_slice_in_dim(vg, rs, kr, axis=1)
        qr = lax.dynamic_index_in_dim(qg, r, axis=1, keepdims=False)
        s_lat = jnp.einsum('bqhd,bikhd->bhqik', qr, kb, preferred_element_type=jnp.float32) * scale
        dr_idx = rs + jnp.arange(kr) - r + WIN_R - 1
        bias = rpb32[:, dr_idx[:, None, None], dc_idx[None, :, :]].transpose(0, 2, 1, 3)
        s_lat = jnp.where(col_mask[:, None, :], s_lat + bias, NEG_INF)
        s_ctx = jnp.einsum('bqhd,bchd->bhqc', qr, k_ctx, preferred_element_type=jnp.float32) * scale
        s = jnp.concatenate([s_lat.reshape(b, h, GRID_W, kr * GRID_W), s_ctx], axis=-1)
        p = jax.nn.softmax(s, axis=-1).astype(v.dtype)
        p_lat = p[..., :kr * GRID_W].reshape(b, h, GRID_W, kr, GRID_W)
        p_ctx = p[..., kr * GRID_W:]
        return jnp.einsum('bhqik,bikhd->bqhd', p_lat, vb) + jnp.einsum('bhqc,bchd->bqhd', p_ctx, v_ctx)

    o = lax.map(one_row, jnp.arange(rows))
    return o.transpose(1, 0, 2, 3, 4).reshape(b, l, h, d)


def ab_heads(h, w_in, q_norm, k_norm):
    b, l, _ = h.shape
    qa, ka, va, ga, ub, vb, gb = split_cols(h @ w_in, AB_SIZES)
    heads = lambda t: t.reshape(b, l, N_HEADS, HEAD_DIM)
    return rms_norm(heads(qa), q_norm), rms_norm(heads(ka), k_norm), heads(va), ga, ub, vb, gb


def spatial_gating(u, v, sg_norm, sg_w, sg_b):
    b, l, _ = u.shape
    n = l // SG_CHUNK
    u = jax.nn.gelu(u)
    v = rms_norm(jax.nn.gelu(v), sg_norm).reshape(b, n, SG_CHUNK, N_HEADS, HEAD_DIM)
    mixed = jnp.einsum('gts,bnsgc->bntgc', sg_w, v) + sg_b.T[None, None, :, :, None]
    return u * mixed.reshape(b, l, BRANCH_W)


def merge(o1, g1, o2, g2):
    b, l = g1.shape[0], g1.shape[1]
    return jnp.concatenate([o1.reshape(b, l, -1) * jax.nn.silu(g1), o2.reshape(b, l, -1) * jax.nn.silu(g2)], axis=-1)


def hgrn_gates(z, lb):
    lbf = lb.astype(jnp.float32)
    f = lbf + (1 - lbf) * jax.nn.sigmoid(z.astype(jnp.float32))
    return jnp.log(f), 1 - f


def hgrn_chunk_scan(q, k, v, logf, s0):
    b, l, h, _ = q.shape
    nc = l // HGRN_CHUNK
    to_chunks = lambda t: t.reshape(b, nc, HGRN_CHUNK, h, t.shape[-1]).transpose(1, 0, 3, 2, 4).astype(jnp.float32)
    lower = jnp.tril(jnp.ones((HGRN_CHUNK, HGRN_CHUNK), dtype=bool))

    def step(s, inp):
        qi, ki, vi, gi = inp
        a = jnp.cumsum(gi, axis=2)
        diff = jnp.where(lower[None, None, :, :, None], a[:, :, :, None, :] - a[:, :, None, :, :], -jnp.inf)
        p = jnp.einsum('bhtk,bhtsk,bhsk->bhts', qi, jnp.exp(diff), ki)
        o = jnp.einsum('bhts,bhsv->bhtv', p, vi) + jnp.einsum('bhtk,bhkv->bhtv', qi * jnp.exp(a), s)
        a_last = a[:, :, -1:, :]
        s = jnp.exp(a_last[:, :, 0, :])[..., None] * s + jnp.einsum('bhsk,bhsv->bhkv', ki * jnp.exp(a_last - a), vi)
        return s, o

    s_fin, o = lax.scan(step, s0.astype(jnp.float32), (to_chunks(q), to_chunks(k), to_chunks(v), to_chunks(logf)))
    return o.transpose(1, 0, 3, 2, 4).reshape(b, l, h, v.shape[-1]), s_fin


def hgrn_mixer(qc, f_fwd, f_bwd, ic, lb_f, lb_b, s0_f, s0_b, out_norm):
    b, l, _ = qc.shape
    heads = lambda t: t.reshape(b, l, N_HEADS, -1)
    flip = lambda t: jnp.flip(t, axis=1)
    lf_f, k_f = hgrn_gates(f_fwd, lb_f)
    lf_b, k_b = hgrn_gates(f_bwd, lb_b)
    q, i = heads(qc), heads(ic)
    o_f, s_f = hgrn_chunk_scan(q, heads(k_f), i, heads(lf_f), s0_f)
    o_b, s_b = hgrn_chunk_scan(flip(q), flip(heads(k_b)), flip(i), flip(heads(lf_b)), s0_b)
    o = rms_norm(o_f + flip(o_b), out_norm).astype(qc.dtype)
    return o.reshape(b, l, N_HEADS * HGRN_DV), jnp.stack([s_f, s_b], axis=1)


def mla_queries(cq, q_a_norm, w_q_up, q_gain, rotate):
    b, l, _ = cq.shape
    q = (rms_norm(cq, q_a_norm) @ w_q_up).reshape(b, l, N_HEADS, QK_D)
    q_nope = rms_norm(q[..., :NOPE_D], q_gain[:NOPE_D])
    q_rope = rms_norm(q[..., NOPE_D:], q_gain[NOPE_D:])
    if rotate:
        q_rope = axial_rope(q_rope)
    return jnp.concatenate([q_nope, q_rope], axis=-1)


def mla_keys(ckv_n, kr_n, w_kv_up, k_gain, rotate):
    b, l, _ = ckv_n.shape
    kv = (ckv_n @ w_kv_up).reshape(b, l, N_HEADS, NOPE_D + V_D)
    k_nope = rms_norm(kv[..., :NOPE_D], k_gain[:NOPE_D])
    k_rope = kr_n[:, :, None, :]
    if rotate:
        k_rope = axial_rope(k_rope)
    k = jnp.concatenate([k_nope, jnp.broadcast_to(k_rope, (b, l, N_HEADS, ROPE_D))], axis=-1)
    return k, kv[..., NOPE_D:]


def setup_inputs(seed: int = 0) -> dict:
    key = jax.random.key(seed)
    keys = iter(jax.random.split(key, 29))

    def nrm(shape, scale=1.0):
        return scale * jax.random.normal(next(keys), shape, jnp.float32)

    def gain(shape):
        return 1.0 + 0.1 * jax.random.normal(next(keys), shape, jnp.float32)

    return {
        'x_prompt': nrm((BATCH, SEQ, D_MODEL)),
        'x_sample': nrm((DEC_BATCH, DEC_SEQ, D_MODEL)),
        'cache_na_k': nrm((DEC_BATCH, N_AB, PAST_LEN, N_HEADS, HEAD_DIM)),
        'cache_na_v': nrm((DEC_BATCH, N_AB, PAST_LEN, N_HEADS, HEAD_DIM)),
        'state_hgrn': nrm((DEC_BATCH, N_CD, 2, N_HEADS, HGRN_DK, HGRN_DV), 0.5),
        'cache_mla_ckv': nrm((DEC_BATCH, N_CD, PAST_LEN, KV_LORA)),
        'cache_mla_krope': nrm((DEC_BATCH, N_CD, PAST_LEN, ROPE_D)),
        'c': nrm((DEC_BATCH, D_MODEL)),
        'c_ctx': nrm((D_MODEL,)),
        'norm_w': gain((DEPTH, D_MODEL)),
        'w_ada': nrm((DEPTH, D_MODEL, 3 * D_MODEL), 0.5 * D_MODEL ** -0.5),
        'b_ada': nrm((DEPTH, 3 * D_MODEL), 0.02),
        'w_out': nrm((DEPTH, MIX_W, D_MODEL), MIX_W ** -0.5),
        'w_in_ab': nrm((N_AB, D_MODEL, AB_IN), D_MODEL ** -0.5),
        'na_q_norm': gain((N_AB, HEAD_DIM)),
        'na_k_norm': gain((N_AB, HEAD_DIM)),
        'na_rpb': nrm((N_AB, N_HEADS, 2 * WIN_R - 1, 2 * WIN_C - 1), 0.1),
        'sg_norm': gain((N_AB, BRANCH_W)),
        'sg_w': nrm((N_AB, N_HEADS, SG_CHUNK, SG_CHUNK), SG_CHUNK ** -0.5),
        'sg_b': nrm((N_AB, N_HEADS, SG_CHUNK), 0.02),
        'w_in_cd': nrm((N_CD, D_MODEL, CD_IN), D_MODEL ** -0.5),
        'hgrn_lb': nrm((DEPTH, 2, C_K)),
        'hgrn_out_norm': gain((N_CD, HGRN_DV)),
        'mla_q_a_norm': gain((N_CD, Q_LORA)),
        'mla_w_q_up': nrm((N_CD, Q_LORA, N_HEADS * QK_D), Q_LORA ** -0.5),
        'mla_kv_a_norm': gain((N_CD, KV_LORA)),
        'mla_w_kv_up': nrm((N_CD, KV_LORA, N_HEADS * (NOPE_D + V_D)), KV_LORA ** -0.5),
        'mla_q_norm': gain((N_CD, QK_D)),
        'mla_k_norm': gain((N_CD, QK_D)),
    }


def reference(x_prompt, x_sample, cache_na_k, cache_na_v, state_hgrn, cache_mla_ckv, cache_mla_krope, c, c_ctx,
              norm_w, w_ada, b_ada, w_out,
              w_in_ab, na_q_norm, na_k_norm, na_rpb, sg_norm, sg_w, sg_b,
              w_in_cd, hgrn_lb, hgrn_out_norm, mla_q_a_norm, mla_w_q_up, mla_kv_a_norm, mla_w_kv_up,
              mla_q_norm, mla_k_norm):
    lb_cum = jnp.cumsum(jax.nn.softmax(hgrn_lb.astype(jnp.float32), axis=0), axis=0)
    lower_bounds = lb_cum - lb_cum[:1]
    cond_ctx = c_ctx[None, None, :]
    cond_lat = c[:, None, :]
    bp = x_prompt.shape[0]
    xp, xs = x_prompt, x_sample
    na_k_new, na_v_new, hgrn_new, ckv_new, kr_new = [], [], [], [], []
    for layer in range(DEPTH):
        j = layer // 2
        hp, gate_p = modulate(xp, norm_w[layer], cond_ctx, w_ada[layer], b_ada[layer])
        hs, gate_s = modulate(xs, norm_w[layer], cond_lat, w_ada[layer], b_ada[layer])
        if layer % 2 == 0:
            qa, ka, va, ga, ub, vb, gb = ab_heads(hp, w_in_ab[j], na_q_norm[j], na_k_norm[j])
            oa = block_attend(qa, ka, va)
            ob = spatial_gating(ub, vb, sg_norm[j], sg_w[j], sg_b[j])
            mix_p = merge(oa, ga, ob, gb)
            na_k_new.append(ka)
            na_v_new.append(va)
            qa, ka, va, ga, ub, vb, gb = ab_heads(hs, w_in_ab[j], na_q_norm[j], na_k_norm[j])
            oa = na_latent(qa, ka, va, cache_na_k[:, j], cache_na_v[:, j], na_rpb[j])
            ob = spatial_gating(ub, vb, sg_norm[j], sg_w[j], sg_b[j])
            mix_s = merge(oa, ga, ob, gb)
        else:
            lb_f, lb_b = lower_bounds[layer, 0], lower_bounds[layer, 1]
            qc, ffw, fbw, ic, gc, cq, ckv, kr, gd = split_cols(hp @ w_in_cd[j], CD_SIZES)
            zeros = jnp.zeros((bp, N_HEADS, HGRN_DK, HGRN_DV), jnp.float32)
            oc, st = hgrn_mixer(qc, ffw, fbw, ic, lb_f, lb_b, zeros, zeros, hgrn_out_norm[j])
            q = mla_queries(cq, mla_q_a_norm[j], mla_w_q_up[j], mla_q_norm[j], False)
            ckv_n = rms_norm(ckv, mla_kv_a_norm[j])
            kr_n = rms_norm(kr, mla_k_norm[j][NOPE_D:])
            k, v = mla_keys(ckv_n, kr_n, mla_w_kv_up[j], mla_k_norm[j], False)
            od = block_attend(q, k, v)
            mix_p = merge(oc, gc, od, gd)
            hgrn_new.append(st)
            ckv_new.append(ckv_n)
            kr_new.append(kr_n)
            qc, ffw, fbw, ic, gc, cq, ckv, kr, gd = split_cols(hs @ w_in_cd[j], CD_SIZES)
            oc, _ = hgrn_mixer(qc, ffw, fbw, ic, lb_f, lb_b, state_hgrn[:, j, 0], state_hgrn[:, j, 1], hgrn_out_norm[j])
            q = mla_queries(cq, mla_q_a_norm[j], mla_w_q_up[j], mla_q_norm[j], True)
            k_lat, v_lat = mla_keys(rms_norm(ckv, mla_kv_a_norm[j]), rms_norm(kr, mla_k_norm[j][NOPE_D:]),
                                    mla_w_kv_up[j], mla_k_norm[j], True)
            k_ctx, v_ctx = mla_keys(cache_mla_ckv[:, j], cache_mla_krope[:, j], mla_w_kv_up[j], mla_k_norm[j], False)
            od = block_attend(q, jnp.concatenate([k_ctx, k_lat], axis=1), jnp.concatenate([v_ctx, v_lat], axis=1))
            mix_s = merge(oc, gc, od, gd)
        xp = xp + gate_p * (mix_p @ w_out[layer])
        xs = xs + gate_s * (mix_s @ w_out[layer])
    new_na_k = jnp.stack(na_k_new, axis=1)
    new_na_v = jnp.stack(na_v_new, axis=1)
    new_hgrn_state = jnp.stack(hgrn_new, axis=1)
    new_mla_ckv = jnp.stack(ckv_new, axis=1)
    new_mla_krope = jnp.stack(kr_new, axis=1)
    return (xp, xs, new_na_k, new_na_v, new_hgrn_state, new_mla_ckv, new_mla_krope)
```

```python
import functools

import numpy as np
import jax
import jax.numpy as jnp
from jax import lax
from jax.experimental import pallas as pl
from jax.experimental.pallas import tpu as pltpu

F32 = jnp.float32
BF16 = jnp.bfloat16

GRID_W = 64
WIN_R = 8
WIN_C = 16
SG_CHUNK = 128
NOPE_D = 128
ROPE_D = 64
ROPE_BASE = 10000.0
EPS = 1e-6
NEG_INF = -1e30

LANES = 128
VMEM_LIMIT = 56 * 1024 * 1024

ROW_TILE = 512
COL_TILE = 1024
NA_ROWS = 4
NA_SPAN = NA_ROWS + WIN_R
ATTN_Q = 256
SCAN_BLOCK = 256
SCAN_CHUNK = 32
EXP_CLAMP = 80.0


def _cparams(n_axes):
    return pltpu.CompilerParams(
        dimension_semantics=("arbitrary",) * n_axes, vmem_limit_bytes=VMEM_LIMIT)


def _pick_tile(n, pref):
    t = pref
    while n % t:
        t //= 2
    assert t >= LANES or t == n, (n, pref)
    return t


def _rms(x, w):
    return x * lax.rsqrt(jnp.mean(x * x, axis=-1, keepdims=True) + EPS) * w


def _silu(x):
    return x * jax.nn.sigmoid(x)


def _gelu(x):
    return 0.5 * x * (1.0 + jnp.tanh(0.7978845608028654 * (x + 0.044715 * (x * x * x))))


def _dot(a, b):
    return jnp.dot(a, b, preferred_element_type=F32)


def _dot_nt(a, b):
    return lax.dot_general(a, b, (((1,), (1,)), ((), ())), preferred_element_type=F32)


def _dot_tn(a, b):
    return lax.dot_general(a, b, (((0,), (0,)), ((), ())), preferred_element_type=F32)


def _mod_kernel(c_ref, w_ref, b_ref, o_ref):
    a = _silu(c_ref[...]).astype(BF16)
    o_ref[0] = _dot(a, w_ref[0].astype(BF16)) + b_ref[0]


def _modulation(cond, w_ada, b_ada):
    depth, d, n = w_ada.shape
    r = cond.shape[0]
    tn = _pick_tile(n, 512)
    return pl.pallas_call(
        _mod_kernel,
        grid=(depth, n // tn),
        in_specs=[
            pl.BlockSpec((r, d), lambda l, j: (0, 0)),
            pl.BlockSpec((1, d, tn), lambda l, j: (l, 0, j)),
            pl.BlockSpec((1, 1, tn), lambda l, j: (l, 0, j)),
        ],
        out_specs=pl.BlockSpec((1, r, tn), lambda l, j: (l, 0, j)),
        out_shape=jax.ShapeDtypeStruct((depth, r, n), F32),
        compiler_params=_cparams(2),
        name="adaln_modulation",
    )(cond, w_ada, b_ada.reshape(depth, 1, n))


def _in_kernel(x_ref, nw_ref, sc_ref, sh_ref, w_ref, o_ref, h_ref):
    @pl.when(pl.program_id(1) == 0)
    def _():
        h = _rms(x_ref[...], nw_ref[...])
        h_ref[...] = (h * (1.0 + sc_ref[0]) + sh_ref[0]).astype(BF16)

    o_ref[...] = _dot(h_ref[...], w_ref[...]).astype(o_ref.dtype)


def _in_proj(x2, seq, norm_w, scale, shift, w):
    m, d = x2.shape
    n = w.shape[1]
    nb = scale.shape[0]
    tm = _pick_tile(m, ROW_TILE)
    tn = _pick_tile(n, COL_TILE)
    if nb == 1:
        cond_map = lambda i, j: (0, 0, 0)
    else:
        assert seq % tm == 0
        cond_map = lambda i, j: ((i * tm) // seq, 0, 0)
    return pl.pallas_call(
        _in_kernel,
        grid=(m // tm, n // tn),
        in_specs=[
            pl.BlockSpec((tm, d), lambda i, j: (i, 0)),
            pl.BlockSpec((1, d), lambda i, j: (0, 0)),
            pl.BlockSpec((1, 1, d), cond_map),
            pl.BlockSpec((1, 1, d), cond_map),
            pl.BlockSpec((d, tn), lambda i, j: (0, j)),
        ],
        out_specs=pl.BlockSpec((tm, tn), lambda i, j: (i, j)),
        out_shape=jax.ShapeDtypeStruct((m, n), BF16),
        scratch_shapes=[pltpu.VMEM((tm, d), BF16)],
        compiler_params=_cparams(2),
        name="norm_mod_in_proj",
    )(x2, norm_w.reshape(1, d), scale, shift, w)


def _out_kernel(a_ref, b_ref, wa_ref, wb_ref, x_ref, g_ref, o_ref):
    acc = _dot(a_ref[...], wa_ref[...]) + _dot(b_ref[...], wb_ref[...])
    o_ref[...] = x_ref[...] + g_ref[0] * acc


def _out_proj(mix_a, mix_b, w_out, x2, seq, gate):
    m, d = x2.shape
    wdt = mix_a.shape[1]
    nb = gate.shape[0]
    tm = _pick_tile(m, ROW_TILE)
    tn = _pick_tile(d, COL_TILE)
    if nb == 1:
        cond_map = lambda i, j: (0, 0, j)
    else:
        assert seq % tm == 0
        cond_map = lambda i, j: ((i * tm) // seq, 0, j)
    return pl.pallas_call(
        _out_kernel,
        grid=(m // tm, d // tn),
        in_specs=[
            pl.BlockSpec((tm, wdt), lambda i, j: (i, 0)),
            pl.BlockSpec((tm, wdt), lambda i, j: (i, 0)),
            pl.BlockSpec((wdt, tn), lambda i, j: (0, j)),
            pl.BlockSpec((wdt, tn), lambda i, j: (1, j)),
            pl.BlockSpec((tm, tn), lambda i, j: (i, j)),
            pl.BlockSpec((1, 1, tn), cond_map),
        ],
        out_specs=pl.BlockSpec((tm, tn), lambda i, j: (i, j)),
        out_shape=jax.ShapeDtypeStruct((m, d), F32),
        compiler_params=_cparams(2),
        name="out_proj_residual",
    )(mix_a, mix_b, w_out, w_out, x2, gate)


def _attn_ctx0_kernel(q_ref, k_ref, v_ref, g_ref, qw_ref, kw_ref, o_ref, ko_ref, vo_ref):
    hd = q_ref.shape[-1]
    qn = _rms(q_ref[...].astype(F32), qw_ref[...]) * (hd ** -0.5)
    kn = _rms(k_ref[...].astype(F32), kw_ref[...])
    v = v_ref[...]
    ko_ref[...] = kn
    vo_ref[...] = v.astype(F32)
    s = _dot_nt(qn.astype(BF16), kn.astype(BF16))
    p = jnp.exp(s - jnp.max(s, axis=-1, keepdims=True))
    l = jnp.sum(p, axis=-1, keepdims=True)
    o = _dot(p.astype(BF16), v) / l
    o_ref[...] = (o * _silu(g_ref[...].astype(F32))).astype(o_ref.dtype)


def _attn_ctx0(proj, batch, seq, heads, q_norm, k_norm):
    hd = LANES
    m = batch * seq
    blk = lambda off: pl.BlockSpec((seq, hd), lambda b, h: (b, off * heads + h))
    vec = pl.BlockSpec((1, hd), lambda b, h: (0, 0))
    return pl.pallas_call(
        _attn_ctx0_kernel,
        grid=(batch, heads),
        in_specs=[blk(0), blk(1), blk(2), blk(3), vec, vec],
        out_specs=[blk(0), blk(0), blk(0)],
        out_shape=[
            jax.ShapeDtypeStruct((m, heads * hd), BF16),
            jax.ShapeDtypeStruct((m, heads * hd), F32),
            jax.ShapeDtypeStruct((m, heads * hd), F32),
        ],
        compiler_params=_cparams(2),
        name="ctx_attention_l0",
    )(proj, proj, proj, proj, q_norm.reshape(1, hd), k_norm.reshape(1, hd))


def _na_bias_tables(rpb, rows):
    tiles = rows // NA_ROWS
    kr_n = min(WIN_R, rows)
    assert kr_n == WIN_R and tiles >= 3
    t_var = np.array([0, 1, tiles - 1])
    ks = np.clip(NA_ROWS * t_var - WIN_R // 2, 0, rows - NA_SPAN)
    r = NA_ROWS * t_var[:, None] + np.arange(NA_ROWS)[None, :]
    rs = np.clip(r - WIN_R // 2, 0, rows - WIN_R)
    kr = ks[:, None] + np.arange(NA_SPAN)[None, :]
    row_ok = (kr[:, None, :] >= rs[:, :, None]) & (kr[:, None, :] < rs[:, :, None] + WIN_R)
    dr = np.clip(kr[:, None, :] - r[:, :, None] + WIN_R - 1, 0, 2 * WIN_R - 2)
    cols = np.arange(GRID_W)
    cstart = np.clip(cols - WIN_C // 2, 0, GRID_W - WIN_C)
    col_ok = (cols[None, :] >= cstart[:, None]) & (cols[None, :] < cstart[:, None] + WIN_C)
    dc = np.clip(cols[None, :] - cols[:, None] + WIN_C - 1, 0, 2 * WIN_C - 2)
    b = rpb.astype(F32)[:, dr[:, :, None, :, None], dc[None, None, :, None, :]]
    ok = row_ok[:, :, None, :, None] & col_ok[None, None, :, None, :]
    b = jnp.where(ok[None], b, NEG_INF)
    return b.reshape(rpb.shape[0], 3, NA_ROWS * GRID_W, NA_SPAN * GRID_W)


def _na_kernel(q_ref, k_ref, v_ref, g_ref, kc_ref, vc_ref, b_ref, qw_ref, kw_ref, o_ref, kn_ref, *, rows):
    t = pl.program_id(2)
    hd = q_ref.shape[-1]

    @pl.when(t == 0)
    def _():
        kn_ref[...] = _rms(k_ref[...].astype(F32), kw_ref[...]).astype(BF16)

    qn = (_rms(q_ref[...].astype(F32), qw_ref[...]) * (hd ** -0.5)).astype(BF16)
    ks = jnp.clip(NA_ROWS * t - WIN_R // 2, 0, rows - NA_SPAN)
    start = pl.multiple_of(ks * GRID_W, GRID_W)
    k_lat = kn_ref[pl.ds(start, NA_SPAN * GRID_W), :]
    v_lat = v_ref[pl.ds(start, NA_SPAN * GRID_W), :]
    s_lat = _dot_nt(qn, k_lat) + b_ref[0, 0]
    s_ctx = _dot_nt(qn, kc_ref[0].astype(BF16))
    mx = jnp.maximum(jnp.max(s_lat, axis=-1, keepdims=True), jnp.max(s_ctx, axis=-1, keepdims=True))
    p_lat = jnp.exp(s_lat - mx)
    p_ctx = jnp.exp(s_ctx - mx)
    l = jnp.sum(p_lat, axis=-1, keepdims=True) + jnp.sum(p_ctx, axis=-1, keepdims=True)
    o = (_dot(p_lat.astype(BF16), v_lat) + _dot(p_ctx.astype(BF16), vc_ref[0].astype(BF16))) / l
    o_ref[...] = (o * _silu(g_ref[...].astype(F32))).astype(o_ref.dtype)


def _na_latent(proj, batch, seq, heads, k_ctx, v_ctx, bias, q_norm, k_norm):
    hd = LANES
    rows = seq // GRID_W
    tiles = rows // NA_ROWS
    tq = NA_ROWS * GRID_W
    lc = k_ctx.shape[1]
    qblk = lambda off: pl.BlockSpec((tq, hd), lambda b, h, t: (b * tiles + t, off * heads + h))
    full = lambda off: pl.BlockSpec((seq, hd), lambda b, h, t: (b, off * heads + h))
    ctx = pl.BlockSpec((1, lc, hd), lambda b, h, t: (b, 0, h))
    variant = lambda t: jnp.where(t == 0, 0, jnp.where(t == tiles - 1, 2, 1))
    vec = pl.BlockSpec((1, hd), lambda b, h, t: (0, 0))
    return pl.pallas_call(
        functools.partial(_na_kernel, rows=rows),
        grid=(batch, heads, tiles),
        in_specs=[
            qblk(0), full(1), full(2), qblk(3), ctx, ctx,
            pl.BlockSpec((1, 1, tq, NA_SPAN * GRID_W), lambda b, h, t: (h, variant(t), 0, 0)),
            vec, vec,
        ],
        out_specs=qblk(0),
        out_shape=jax.ShapeDtypeStruct((batch * seq, heads * hd), BF16),
        scratch_shapes=[pltpu.VMEM((seq, hd), BF16)],
        compiler_params=_cparams(3),
        name="neighbourhood_attention",
    )(proj, proj, proj, proj, k_ctx, v_ctx, bias, q_norm.reshape(1, hd), k_norm.reshape(1, hd))


def _sg_kernel(u_ref, v_ref, g_ref, nw_ref, w_ref, b_ref, o_ref):
    groups = w_ref.shape[0]
    hd = w_ref.shape[-1]
    u = _gelu(u_ref[...].astype(F32))
    v = _rms(_gelu(v_ref[...].astype(F32)), nw_ref[...]).astype(BF16)
    g = _silu(g_ref[...].astype(F32))
    for gi in range(groups):
        sl = slice(gi * hd, (gi + 1) * hd)
        mixed = _dot(w_ref[gi], v[:, sl]) + b_ref[gi]
        o_ref[:, sl] = (u[:, sl] * mixed * g[:, sl]).astype(o_ref.dtype)


def _spatial_gating(proj, width, sg_norm, sg_w, sg_b):
    m = proj.shape[0]
    groups = sg_w.shape[0]
    blk = lambda off: pl.BlockSpec((SG_CHUNK, width), lambda i: (i, off))
    bias = jnp.broadcast_to(sg_b.astype(F32)[:, :, None], (groups, SG_CHUNK, LANES))
    return pl.pallas_call(
        _sg_kernel,
        grid=(m // SG_CHUNK,),
        in_specs=[
            blk(4), blk(5), blk(6),
            pl.BlockSpec((1, width), lambda i: (0, 0)),
            pl.BlockSpec((groups, SG_CHUNK, SG_CHUNK), lambda i: (0, 0, 0)),
            pl.BlockSpec((groups, SG_CHUNK, LANES), lambda i: (0, 0, 0)),
        ],
        out_specs=pl.BlockSpec((SG_CHUNK, width), lambda i: (i, 0)),
        out_shape=jax.ShapeDtypeStruct((m, width), BF16),
        compiler_params=_cparams(1),
        name="spatial_gating",
    )(proj, proj, proj, sg_norm.reshape(1, width), sg_w.astype(BF16), bias)


def _scan_matrices(rev):
    t = np.arange(SCAN_BLOCK)
    same = (t[:, None] // SCAN_CHUNK) == (t[None, :] // SCAN_CHUNK)
    pos = t % SCAN_CHUNK
    half = SCAN_CHUNK // 2
    if rev:
        incl = same & (t[None, :] >= t[:, None])
        ref = same & (pos[None, :] >= half)
    else:
        incl = same & (t[None, :] <= t[:, None])
        ref = same & (pos[None, :] < half)
    return jnp.asarray(np.concatenate([incl, ref, same], axis=0), dtype=BF16)


def _hgrn_kernel(*refs, rev, has_s0, emit_state, final):
    refs = list(refs)
    q_ref, z_ref, i_ref, lb_ref, tri_ref = refs[:5]
    del refs[:5]
    s0_ref = refs.pop(0) if has_s0 else None
    if final:
        of_ref, gate_ref, onw_ref = refs[:3]
        del refs[:3]
    o_ref = refs.pop(0)
    st_ref = refs.pop(0) if emit_state else None
    state_ref = refs.pop(0)
    t = pl.program_id(2)
    tb = q_ref.shape[0]
    nc = tb // SCAN_CHUNK

    @pl.when(t == 0)
    def _():
        if has_s0:
            state_ref[...] = s0_ref[0, 0].T
        else:
            state_ref[...] = jnp.zeros_like(state_ref)

    q = q_ref[...].astype(F32)
    v = i_ref[...]
    lb = lb_ref[...]
    f = lb + (1.0 - lb) * jax.nn.sigmoid(z_ref[...].astype(F32))
    g = jnp.log(f)
    k = 1.0 - f
    tri = tri_ref[...]
    g_hi = g.astype(BF16)
    r1 = g - g_hi.astype(F32)
    g_mid = r1.astype(BF16)
    g_lo = (r1 - g_mid.astype(F32)).astype(BF16)
    sums = _dot(tri, g_hi) + _dot(tri, g_mid) + _dot(tri, g_lo)
    a = sums[:tb]
    a_ref = sums[tb:2 * tb]
    a_tot = sums[2 * tb:]
    qe = (q * jnp.exp(jnp.minimum(a - a_ref, EXP_CLAMP))).astype(BF16)
    ke = (k * jnp.exp(jnp.minimum(a_ref - a, EXP_CLAMP))).astype(BF16)
    p = jnp.where(tri[:tb] > 0, _dot_nt(qe, ke), 0.0)
    o = _dot(p.astype(BF16), v)
    qa = (q * jnp.exp(a)).astype(BF16)
    kd = (k * jnp.exp(a_tot - a)).astype(BF16)
    dec = jnp.exp(a_tot)
    state = state_ref[...]
    inter = [None] * nc
    for c in (range(nc - 1, -1, -1) if rev else range(nc)):
        sl = slice(c * SCAN_CHUNK, (c + 1) * SCAN_CHUNK)
        inter[c] = _dot_nt(qa[sl], state.astype(BF16))
        state = state * dec[c * SCAN_CHUNK:c * SCAN_CHUNK + 1] + _dot_tn(v[sl], kd[sl])
    state_ref[...] = state
    o = o + jnp.concatenate(inter, axis=0)

    if final:
        o = _rms(o + of_ref[...], onw_ref[...])
        o_ref[...] = (o * _silu(gate_ref[...].astype(F32))).astype(o_ref.dtype)
    else:
        o_ref[...] = o

    if emit_state:
        @pl.when(t == pl.num_programs(2) - 1)
        def _():
            st_ref[0, 0] = state.T


def _hgrn_pass(proj, batch, seq, heads, lb, s0, *, rev, emit_state, prev=None, out_norm=None):
    hd = LANES
    tb = _pick_tile(seq, SCAN_BLOCK)
    assert tb == SCAN_BLOCK
    nblk = seq // tb
    final = prev is not None
    tmap = (lambda t: nblk - 1 - t) if rev else (lambda t: t)
    blk = lambda off: pl.BlockSpec((tb, hd), lambda b, h, t: (b * nblk + tmap(t), off * heads + h))
    in_specs = [
        blk(0), blk(2 if rev else 1), blk(3),
        pl.BlockSpec((1, hd), lambda b, h, t: (0, h)),
        pl.BlockSpec((3 * tb, tb), lambda b, h, t: (0, 0)),
    ]
    args = [proj, proj, proj, lb.reshape(1, heads * hd), _scan_matrices(rev)]
    if s0 is not None:
        in_specs.append(pl.BlockSpec((1, 1, hd, hd), lambda b, h, t: (b, h, 0, 0)))
        args.append(s0)
    if final:
        in_specs += [
            pl.BlockSpec((tb, hd), lambda b, h, t: (b * nblk + tmap(t), h)),
            blk(4),
            pl.BlockSpec((1, hd), lambda b, h, t: (0, 0)),
        ]
        args += [prev, proj, out_norm.reshape(1, hd)]
    out_specs = [pl.BlockSpec((tb, hd), lambda b, h, t: (b * nblk + tmap(t), h))]
    out_shape = [jax.ShapeDtypeStruct((batch * seq, heads * hd), BF16 if final else F32)]
    if emit_state:
        out_specs.append(pl.BlockSpec((1, 1, hd, hd), lambda b, h, t: (b, h, 0, 0)))
        out_shape.append(jax.ShapeDtypeStruct((batch, heads, hd, hd), F32))
    res = pl.pallas_call(
        functools.partial(_hgrn_kernel, rev=rev, has_s0=s0 is not None, emit_state=emit_state, final=final),
        grid=(batch, heads, nblk),
        in_specs=in_specs,
        out_specs=out_specs,
        out_shape=out_shape,
        scratch_shapes=[pltpu.VMEM((hd, hd), F32)],
        compiler_params=_cparams(3),
        name="hgrn_scan_bwd" if rev else "hgrn_scan_fwd",
    )(*args)
    return res if emit_state else (res[0], None)


_ROPE_SWAP = np.concatenate([np.arange(16, 32), np.arange(0, 16), np.arange(48, 64), np.arange(32, 48)])


def _rope_table(seq, rotate):
    if not rotate:
        row = np.concatenate([np.ones(ROPE_D), np.zeros(ROPE_D)]).astype(np.float32)
        return jnp.asarray(np.broadcast_to(row, (seq, 2 * ROPE_D)))
    t = jnp.arange(seq)
    half = ROPE_D // 4
    inv = ROPE_BASE ** (-jnp.arange(half, dtype=F32) / half)
    ang_r = (t // GRID_W).astype(F32)[:, None] * inv[None, :]
    ang_c = (t % GRID_W).astype(F32)[:, None] * inv[None, :]
    cos = jnp.concatenate([jnp.cos(ang_r)] * 2 + [jnp.cos(ang_c)] * 2, axis=-1)
    sin = jnp.concatenate([-jnp.sin(ang_r), jnp.sin(ang_r), -jnp.sin(ang_c), jnp.sin(ang_c)], axis=-1)
    return jnp.concatenate([cos, sin], axis=-1)


def _rotate_rope(r, tab, lane_lo):
    rot = r * tab
    rot = rot + pltpu.roll(rot, ROPE_D, 1)
    return jnp.where(lane_lo, rot, 0.0)


def _qup_kernel(cq_ref, nw_ref, w_ref, gn_ref, gr_ref, tab_ref, o_ref, h_ref):
    @pl.when(pl.program_id(1) == 0)
    def _():
        h_ref[...] = _rms(cq_ref[...].astype(F32), nw_ref[...]).astype(BF16)

    y = _dot(h_ref[...], w_ref[...])
    hw = NOPE_D + 2 * ROPE_D
    scale = (NOPE_D + ROPE_D) ** -0.5
    lane_lo = lax.broadcasted_iota(jnp.int32, (1, LANES), 1) < ROPE_D
    tab = tab_ref[...] * gr_ref[...]
    for hh in range(y.shape[1] // hw):
        yn = y[:, hh * hw:hh * hw + NOPE_D]
        yr = y[:, hh * hw + NOPE_D:(hh + 1) * hw]
        o_ref[:, hh * hw:hh * hw + NOPE_D] = (_rms(yn, gn_ref[...]) * scale).astype(o_ref.dtype)
        ms = jnp.sum(jnp.where(lane_lo, yr * yr, 0.0), axis=-1, keepdims=True) * (1.0 / ROPE_D)
        qr = _rotate_rope(yr, tab, lane_lo) * (lax.rsqrt(ms + EPS) * scale)
        o_ref[:, hh * hw + NOPE_D:(hh + 1) * hw] = qr.astype(o_ref.dtype)


def _q_up(proj, col_blk, q_lora, seq, a_norm, w, gain_n, gain_r, table):
    m = proj.shape[0]
    n = w.shape[1]
    tm = _pick_tile(seq, ROW_TILE)
    tn = _pick_tile(n, COL_TILE)
    nt = table.shape[0] // tm
    return pl.pallas_call(
        _qup_kernel,
        grid=(m // tm, n // tn),
        in_specs=[
            pl.BlockSpec((tm, q_lora), lambda i, j: (i, col_blk)),
            pl.BlockSpec((1, q_lora), lambda i, j: (0, 0)),
            pl.BlockSpec((q_lora, tn), lambda i, j: (0, j)),
            pl.BlockSpec((1, NOPE_D), lambda i, j: (0, 0)),
            pl.BlockSpec((1, 2 * ROPE_D), lambda i, j: (0, 0)),
            pl.BlockSpec((tm, 2 * ROPE_D), lambda i, j: (i % nt, 0)),
        ],
        out_specs=pl.BlockSpec((tm, tn), lambda i, j: (i, j)),
        out_shape=jax.ShapeDtypeStruct((m, n), BF16),
        scratch_shapes=[pltpu.VMEM((tm, q_lora), BF16)],
        compiler_params=_cparams(2),
        name="mla_q_up",
    )(proj, a_norm.reshape(1, q_lora), w, gain_n.reshape(1, NOPE_D), gain_r.reshape(1, 2 * ROPE_D), table)


def _kvup_kernel(*refs, normalize_in, emit_cache, heads):
    refs = list(refs)
    ckv_ref, kr_ref, nw_ref, gkr_ref, w_ref, gk_ref, tab_ref, k_ref, v_ref = refs[:9]
    lane_lo = lax.broadcasted_iota(jnp.int32, (1, LANES), 1) < ROPE_D
    c = ckv_ref[...].astype(F32)
    r = kr_ref[...].astype(F32)
    if normalize_in:
        c = _rms(c, nw_ref[...])
        ms = jnp.sum(jnp.where(lane_lo, r * r, 0.0), axis=-1, keepdims=True) * (1.0 / ROPE_D)
        r = r * lax.rsqrt(ms + EPS) * gkr_ref[...]
    if emit_cache:
        refs[9][...] = c
        refs[10][...] = r[:, :ROPE_D]
    k_rope = _rotate_rope(r, tab_ref[...], lane_lo).astype(k_ref.dtype)
    y = _dot(c.astype(BF16), w_ref[...])
    hw = 2 * LANES
    for h in range(heads):
        k_ref[:, h * hw:h * hw + NOPE_D] = _rms(y[:, h * hw:h * hw + NOPE_D], gk_ref[...]).astype(k_ref.dtype)
        k_ref[:, h * hw + NOPE_D:(h + 1) * hw] = k_rope
        v_ref[:, h * LANES:(h + 1) * LANES] = y[:, h * hw + NOPE_D:(h + 1) * hw].astype(v_ref.dtype)


def _kv_up(ckv_src, ckv_blk, kr_src, kr_blk, seq, kv_lora, heads, a_norm, gain_kr, w, gain_k, table,
           *, normalize_in, emit_cache):
    m = ckv_src.shape[0]
    tm = _pick_tile(seq, ROW_TILE)
    nt = table.shape[0] // tm
    out_specs = [
        pl.BlockSpec((tm, heads * 2 * LANES), lambda i: (i, 0)),
        pl.BlockSpec((tm, heads * LANES), lambda i: (i, 0)),
    ]
    out_shape = [
        jax.ShapeDtypeStruct((m, heads * 2 * LANES), BF16),
        jax.ShapeDtypeStruct((m, heads * LANES), BF16),
    ]
    if emit_cache:
        out_specs += [pl.BlockSpec((tm, kv_lora), lambda i: (i, 0)), pl.BlockSpec((tm, ROPE_D), lambda i: (i, 0))]
        out_shape += [jax.ShapeDtypeStruct((m, kv_lora), F32), jax.ShapeDtypeStruct((m, ROPE_D), F32)]
    return pl.pallas_call(
        functools.partial(_kvup_kernel, normalize_in=normalize_in, emit_cache=emit_cache, heads=heads),
        grid=(m // tm,),
        in_specs=[
            pl.BlockSpec((tm, kv_lora), lambda i: (i, ckv_blk)),
            pl.BlockSpec((tm, 2 * ROPE_D), lambda i: (i, kr_blk)),
            pl.BlockSpec((1, kv_lora), lambda i: (0, 0)),
            pl.BlockSpec((1, 2 * ROPE_D), lambda i: (0, 0)),
            pl.BlockSpec((kv_lora, heads * 2 * LANES), lambda i: (0, 0)),
            pl.BlockSpec((1, NOPE_D), lambda i: (0, 0)),
            pl.BlockSpec((tm, 2 * ROPE_D), lambda i: (i % nt, 0)),
        ],
        out_specs=out_specs,
        out_shape=out_shape,
        compiler_params=_cparams(1),
        name="mla_kv_up",
    )(ckv_src, kr_src, a_norm.reshape(1, kv_lora), gain_kr.reshape(1, 2 * ROPE_D), w,
      gain_k.reshape(1, NOPE_D), table)


def _mla_attn_kernel(*refs, has_ctx):
    if has_ctx:
        q_ref, k_ref, v_ref, kc_ref, vc_ref, g_ref, o_ref = refs
    else:
        q_ref, k_ref, v_ref, g_ref, o_ref = refs
    q = q_ref[...]
    s = _dot_nt(q, k_ref[...])
    mx = jnp.max(s, axis=-1, keepdims=True)
    if has_ctx:
        s_c = _dot_nt(q, kc_ref[...])
        mx = jnp.maximum(mx, jnp.max(s_c, axis=-1, keepdims=True))
    p = jnp.exp(s - mx)
    l = jnp.sum(p, axis=-1, keepdims=True)
    acc = _dot(p.astype(BF16), v_ref[...])
    if has_ctx:
        p_c = jnp.exp(s_c - mx)
        l = l + jnp.sum(p_c, axis=-1, keepdims=True)
        acc = acc + _dot(p_c.astype(BF16), vc_ref[...])
    o_ref[...] = (acc / l * _silu(g_ref[...].astype(F32))).astype(o_ref.dtype)


def _mla_attention(q, k, v, proj, gate_blk, batch, seq, heads, ctx=None):
    tq = _pick_tile(seq, ATTN_Q)
    nq = seq // tq
    qmap = lambda b, h, t: (b * nq + t, h)
    in_specs = [
        pl.BlockSpec((tq, 2 * LANES), qmap),
        pl.BlockSpec((seq, 2 * LANES), lambda b, h, t: (b, h)),
        pl.BlockSpec((seq, LANES), lambda b, h, t: (b, h)),
    ]
    args = [q, k, v]
    if ctx is not None:
        k_c, v_c = ctx
        lc = k_c.shape[0] // batch
        in_specs += [
            pl.BlockSpec((lc, 2 * LANES), lambda b, h, t: (b, h)),
            pl.BlockSpec((lc, LANES), lambda b, h, t: (b, h)),
        ]
        args += [k_c, v_c]
    in_specs.append(pl.BlockSpec((tq, LANES), lambda b, h, t: (b * nq + t, gate_blk + h)))
    args.append(proj)
    return pl.pallas_call(
        functools.partial(_mla_attn_kernel, has_ctx=ctx is not None),
        grid=(batch, heads, nq),
        in_specs=in_specs,
        out_specs=pl.BlockSpec((tq, LANES), qmap),
        out_shape=jax.ShapeDtypeStruct((batch * seq, heads * LANES), BF16),
        compiler_params=_cparams(3),
        name="mla_attention",
    )(*args)


def _pad_cols(w, mult):
    pad = (-w.shape[1]) % mult
    return w if pad == 0 else jnp.concatenate([w, jnp.zeros((w.shape[0], pad), w.dtype)], axis=1)


def kernel(x_prompt, x_sample, cache_na_k, cache_na_v, state_hgrn, cache_mla_ckv, cache_mla_krope, c, c_ctx,
           norm_w, w_ada, b_ada, w_out,
           w_in_ab, na_q_norm, na_k_norm, na_rpb, sg_norm, sg_w, sg_b,
           w_in_cd, hgrn_lb, hgrn_out_norm, mla_q_a_norm, mla_w_q_up, mla_kv_a_norm, mla_w_kv_up,
           mla_q_norm, mla_k_norm):
    bp, lp, d = x_prompt.shape
    bs, ls, _ = x_sample.shape
    depth = norm_w.shape[0]
    hd = cache_na_k.shape[-1]
    heads = cache_na_k.shape[-2]
    width = heads * hd
    assert hd == LANES and state_hgrn.shape[-2:] == (hd, hd) and ls % GRID_W == 0
    q_lora = mla_q_a_norm.shape[-1]
    kv_lora = mla_kv_a_norm.shape[-1]
    assert mla_q_norm.shape[-1] == NOPE_D + ROPE_D and cache_mla_krope.shape[-1] == ROPE_D
    assert mla_w_kv_up.shape[-1] == heads * 2 * LANES

    n_cond = bs + 1
    cond_rows = -(-n_cond // 8) * 8
    cond = jnp.concatenate([c, c_ctx[None, :], jnp.zeros((cond_rows - n_cond, d), F32)], axis=0)
    mod = _modulation(cond, w_ada, b_ada).reshape(depth, cond_rows, 3, 1, d)

    lb_soft = jax.nn.softmax(hgrn_lb.astype(F32), axis=0)
    lb_cum = jnp.cumsum(lb_soft, axis=0)
    lower_bounds = lb_cum - lb_cum[:1]

    xp = x_prompt.reshape(bp * lp, d)
    xs = x_sample.reshape(bs * ls, d)
    na_k_new, na_v_new, hgrn_new, ckv_new, kr_new = [], [], [], [], []
    for layer in range(depth):
        j = layer // 2
        shift_s, scale_s, gate_s = (mod[layer, :bs, i] for i in range(3))
        shift_p, scale_p, gate_p = (mod[layer, bs:bs + 1, i] for i in range(3))
        w_o = w_out[layer].astype(BF16)
        if layer % 2 == 0:
            w_in = _pad_cols(w_in_ab[j].astype(BF16), COL_TILE)
            proj_p = _in_proj(xp, lp, norm_w[layer], scale_p, shift_p, w_in)
            proj_s = _in_proj(xs, ls, norm_w[layer], scale_s, shift_s, w_in)
            mix_a_p, k_new, v_new = _attn_ctx0(proj_p, bp, lp, heads, na_q_norm[j], na_k_norm[j])
            na_k_new.append(k_new.reshape(bp, lp, heads, hd))
            na_v_new.append(v_new.reshape(bp, lp, heads, hd))
            bias = _na_bias_tables(na_rpb[j], ls // GRID_W)
            lc = cache_na_k.shape[2]
            mix_a_s = _na_latent(proj_s, bs, ls, heads,
                                 cache_na_k[:, j].reshape(bs, lc, width), cache_na_v[:, j].reshape(bs, lc, width),
                                 bias, na_q_norm[j], na_k_norm[j])
            mix_b_p = _spatial_gating(proj_p, width, sg_norm[j], sg_w[j], sg_b[j])
            mix_b_s = _spatial_gating(proj_s, width, sg_norm[j], sg_w[j], sg_b[j])
        else:
            w_cd = w_in_cd[j]
            kr_off = 5 * width + q_lora + kv_lora
            w_kr = w_cd[:, kr_off:kr_off + ROPE_D]
            w_in = jnp.concatenate([w_cd[:, :kr_off + ROPE_D], w_kr[:, _ROPE_SWAP], w_cd[:, kr_off + ROPE_D:]], axis=1)
            w_in = _pad_cols(w_in.astype(BF16), COL_TILE)
            assert (5 * width) % q_lora == 0 and (5 * width + q_lora) % kv_lora == 0
            cq_blk = (5 * width) // q_lora
            ckv_blk = (5 * width + q_lora) // kv_lora
            kr_blk = kr_off // LANES
            gd_blk = kr_blk + 1
            proj_p = _in_proj(xp, lp, norm_w[layer], scale_p, shift_p, w_in)
            proj_s = _in_proj(xs, ls, norm_w[layer], scale_s, shift_s, w_in)

            lb_f, lb_b = lower_bounds[layer, 0], lower_bounds[layer, 1]
            of_p, st_f = _hgrn_pass(proj_p, bp, lp, heads, lb_f, None, rev=False, emit_state=True)
            mix_a_p, st_b = _hgrn_pass(proj_p, bp, lp, heads, lb_b, None, rev=True, emit_state=True,
                                       prev=of_p, out_norm=hgrn_out_norm[j])
            hgrn_new.append(jnp.stack([st_f, st_b], axis=1))
            of_s, _ = _hgrn_pass(proj_s, bs, ls, heads, lb_f, state_hgrn[:, j, 0], rev=False, emit_state=False)
            mix_a_s, _ = _hgrn_pass(proj_s, bs, ls, heads, lb_b, state_hgrn[:, j, 1], rev=True, emit_state=False,
                                    prev=of_s, out_norm=hgrn_out_norm[j])

            wq = mla_w_q_up[j].reshape(q_lora, heads, NOPE_D + ROPE_D)
            wq = jnp.concatenate([wq, wq[:, :, NOPE_D:][:, :, _ROPE_SWAP]], axis=-1)
            wq = wq.reshape(q_lora, heads * 2 * LANES).astype(BF16)
            wkv = mla_w_kv_up[j].astype(BF16)
            gq, gk = mla_q_norm[j], mla_k_norm[j]
            gq_r = jnp.concatenate([gq[NOPE_D:], gq[NOPE_D:][_ROPE_SWAP]])
            gk_r = jnp.concatenate([gk[NOPE_D:], gk[NOPE_D:][_ROPE_SWAP]])
            tab_p = _rope_table(_pick_tile(lp, ROW_TILE), False)
            tab_s = _rope_table(ls, True)
            tab_c = _rope_table(_pick_tile(cache_mla_ckv.shape[2], ROW_TILE), False)

            q_p = _q_up(proj_p, cq_blk, q_lora, lp, mla_q_a_norm[j], wq, gq[:NOPE_D], gq_r, tab_p)
            k_p, v_p, ckv_n, kr_n = _kv_up(proj_p, ckv_blk, proj_p, kr_blk, lp, kv_lora, heads, mla_kv_a_norm[j],
                                           gk_r, wkv, gk[:NOPE_D], tab_p, normalize_in=True, emit_cache=True)
            ckv_new.append(ckv_n.reshape(bp, lp, kv_lora))
            kr_new.append(kr_n.reshape(bp, lp, ROPE_D))
            mix_b_p = _mla_attention(q_p, k_p, v_p, proj_p, gd_blk, bp, lp, heads)

            q_s = _q_up(proj_s, cq_blk, q_lora, ls, mla_q_a_norm[j], wq, gq[:NOPE_D], gq_r, tab_s)
            k_s, v_s = _kv_up(proj_s, ckv_blk, proj_s, kr_blk, ls, kv_lora, heads, mla_kv_a_norm[j],
                              gk_r, wkv, gk[:NOPE_D], tab_s, normalize_in=True, emit_cache=False)
            lc = cache_mla_ckv.shape[2]
            kr_c = cache_mla_krope[:, j].reshape(bs * lc, ROPE_D)
            kr_c = jnp.concatenate([kr_c, jnp.zeros_like(kr_c)], axis=1)
            k_c, v_c = _kv_up(cache_mla_ckv[:, j].reshape(bs * lc, kv_lora), 0, kr_c, 0, lc, kv_lora, heads,
                              mla_kv_a_norm[j], gk_r, wkv, gk[:NOPE_D], tab_c, normalize_in=False, emit_cache=False)
            mix_b_s = _mla_attention(q_s, k_s, v_s, proj_s, gd_blk, bs, ls, heads, ctx=(k_c, v_c))
        xp = _out_proj(mix_a_p, mix_b_p, w_o, xp, lp, gate_p)
        xs = _out_proj(mix_a_s, mix_b_s, w_o, xs, ls, gate_s)
    return (xp.reshape(bp, lp, d), xs.reshape(bs, ls, d),
            jnp.stack(na_k_new, axis=1), jnp.stack(na_v_new, axis=1), jnp.stack(hgrn_new, axis=1),
            jnp.stack(ckv_new, axis=1), jnp.stack(kr_new, axis=1))
```

```python
import functools

import numpy as np
import jax
import jax.numpy as jnp
from jax import lax
from jax.experimental import pallas as pl
from jax.experimental.pallas import tpu as pltpu

F32 = jnp.float32
BF16 = jnp.bfloat16

GRID_W = 64
WIN_R = 8
WIN_C = 16
SG_CHUNK = 128
NOPE_D = 128
ROPE_D = 64
ROPE_BASE = 10000.0
EPS = 1e-6
NEG_INF = -1e30

LANES = 128
VMEM_LIMIT = 56 * 1024 * 1024

ROW_TILE = 512
COL_TILE = 1024
NA_ROWS = 4
NA_SPAN = NA_ROWS + WIN_R
ATTN_Q = 256
SCAN_BLOCK = 256
SCAN_CHUNK = 32
SCAN_HEADS = 8
EXP_CLAMP = 80.0


def _cparams(n_axes):
    return pltpu.CompilerParams(
        dimension_semantics=("arbitrary",) * n_axes, vmem_limit_bytes=VMEM_LIMIT)


def _pick_tile(n, pref):
    t = pref
    while n % t:
        t //= 2
    assert t >= LANES or t == n, (n, pref)
    return t


def _rms(x, w):
    return x * lax.rsqrt(jnp.mean(x * x, axis=-1, keepdims=True) + EPS) * w


def _silu(x):
    return x * jax.nn.sigmoid(x)


def _gelu(x):
    return 0.5 * x * (1.0 + jnp.tanh(0.7978845608028654 * (x + 0.044715 * (x * x * x))))


def _dot(a, b):
    return jnp.dot(a, b, preferred_element_type=F32)


def _dot_nt(a, b):
    return lax.dot_general(a, b, (((1,), (1,)), ((), ())), preferred_element_type=F32)


def _dot_tn(a, b):
    return lax.dot_general(a, b, (((0,), (0,)), ((), ())), preferred_element_type=F32)


def _mod_kernel(c_ref, w_ref, b_ref, o_ref):
    a = _silu(c_ref[...]).astype(BF16)
    o_ref[0] = _dot(a, w_ref[0].astype(BF16)) + b_ref[0]


def _modulation(cond, w_ada, b_ada):
    depth, d, n = w_ada.shape
    r = cond.shape[0]
    tn = _pick_tile(n, 512)
    return pl.pallas_call(
        _mod_kernel,
        grid=(depth, n // tn),
        in_specs=[
            pl.BlockSpec((r, d), lambda l, j: (0, 0)),
            pl.BlockSpec((1, d, tn), lambda l, j: (l, 0, j)),
            pl.BlockSpec((1, 1, tn), lambda l, j: (l, 0, j)),
        ],
        out_specs=pl.BlockSpec((1, r, tn), lambda l, j: (l, 0, j)),
        out_shape=jax.ShapeDtypeStruct((depth, r, n), F32),
        compiler_params=_cparams(2),
        name="adaln_modulation",
    )(cond, w_ada, b_ada.reshape(depth, 1, n))


def _in_kernel(x_ref, nw_ref, sc_ref, sh_ref, w_ref, o_ref, h_ref):
    @pl.when(pl.program_id(1) == 0)
    def _():
        h = _rms(x_ref[...], nw_ref[...])
        h_ref[...] = (h * (1.0 + sc_ref[0]) + sh_ref[0]).astype(BF16)

    o_ref[...] = _dot(h_ref[...], w_ref[...]).astype(o_ref.dtype)


def _in_proj(x2, seq, norm_w, scale, shift, w):
    m, d = x2.shape
    n = w.shape[1]
    nb = scale.shape[0]
    tm = _pick_tile(m, ROW_TILE)
    tn = _pick_tile(n, COL_TILE)
    if nb == 1:
        cond_map = lambda i, j: (0, 0, 0)
    else:
        assert seq % tm == 0
        cond_map = lambda i, j: ((i * tm) // seq, 0, 0)
    return pl.pallas_call(
        _in_kernel,
        grid=(m // tm, n // tn),
        in_specs=[
            pl.BlockSpec((tm, d), lambda i, j: (i, 0)),
            pl.BlockSpec((1, d), lambda i, j: (0, 0)),
            pl.BlockSpec((1, 1, d), cond_map),
            pl.BlockSpec((1, 1, d), cond_map),
            pl.BlockSpec((d, tn), lambda i, j: (0, j)),
        ],
        out_specs=pl.BlockSpec((tm, tn), lambda i, j: (i, j)),
        out_shape=jax.ShapeDtypeStruct((m, n), BF16),
        scratch_shapes=[pltpu.VMEM((tm, d), BF16)],
        compiler_params=_cparams(2),
        name="norm_mod_in_proj",
    )(x2, norm_w.reshape(1, d), scale, shift, w)


def _out_kernel(a_ref, b_ref, wa_ref, wb_ref, x_ref, g_ref, o_ref):
    acc = _dot(a_ref[...], wa_ref[...]) + _dot(b_ref[...], wb_ref[...])
    o_ref[...] = x_ref[...] + g_ref[0] * acc


def _out_proj(mix_a, mix_b, w_out, x2, seq, gate):
    m, d = x2.shape
    wdt = mix_a.shape[1]
    nb = gate.shape[0]
    tm = _pick_tile(m, ROW_TILE)
    tn = _pick_tile(d, COL_TILE)
    if nb == 1:
        cond_map = lambda i, j: (0, 0, j)
    else:
        assert seq % tm == 0
        cond_map = lambda i, j: ((i * tm) // seq, 0, j)
    return pl.pallas_call(
        _out_kernel,
        grid=(m // tm, d // tn),
        in_specs=[
            pl.BlockSpec((tm, wdt), lambda i, j: (i, 0)),
            pl.BlockSpec((tm, wdt), lambda i, j: (i, 0)),
            pl.BlockSpec((wdt, tn), lambda i, j: (0, j)),
            pl.BlockSpec((wdt, tn), lambda i, j: (1, j)),
            pl.BlockSpec((tm, tn), lambda i, j: (i, j)),
            pl.BlockSpec((1, 1, tn), cond_map),
        ],
        out_specs=pl.BlockSpec((tm, tn), lambda i, j: (i, j)),
        out_shape=jax.ShapeDtypeStruct((m, d), F32),
        compiler_params=_cparams(2),
        name="out_proj_residual",
    )(mix_a, mix_b, w_out, w_out, x2, gate)


def _attn_ctx0_kernel(q_ref, k_ref, v_ref, g_ref, qw_ref, kw_ref, o_ref, ko_ref, vo_ref):
    hd = q_ref.shape[-1]
    qn = _rms(q_ref[...].astype(F32), qw_ref[...]) * (hd ** -0.5)
    kn = _rms(k_ref[...].astype(F32), kw_ref[...])
    v = v_ref[...]
    ko_ref[...] = kn
    vo_ref[...] = v.astype(F32)
    s = _dot_nt(qn.astype(BF16), kn.astype(BF16))
    p = jnp.exp(s - jnp.max(s, axis=-1, keepdims=True))
    l = jnp.sum(p, axis=-1, keepdims=True)
    o = _dot(p.astype(BF16), v) / l
    o_ref[...] = (o * _silu(g_ref[...].astype(F32))).astype(o_ref.dtype)


def _attn_ctx0(proj, batch, seq, heads, q_norm, k_norm):
    hd = LANES
    m = batch * seq
    blk = lambda off: pl.BlockSpec((seq, hd), lambda b, h: (b, off * heads + h))
    vec = pl.BlockSpec((1, hd), lambda b, h: (0, 0))
    return pl.pallas_call(
        _attn_ctx0_kernel,
        grid=(batch, heads),
        in_specs=[blk(0), blk(1), blk(2), blk(3), vec, vec],
        out_specs=[blk(0), blk(0), blk(0)],
        out_shape=[
            jax.ShapeDtypeStruct((m, heads * hd), BF16),
            jax.ShapeDtypeStruct((m, heads * hd), F32),
            jax.ShapeDtypeStruct((m, heads * hd), F32),
        ],
        compiler_params=_cparams(2),
        name="ctx_attention_l0",
    )(proj, proj, proj, proj, q_norm.reshape(1, hd), k_norm.reshape(1, hd))


def _na_bias_plan(rows):
    tiles = rows // NA_ROWS
    assert min(WIN_R, rows) == WIN_R and tiles >= 3
    t_var = np.array([0, 1, tiles - 1])
    ks = np.clip(NA_ROWS * t_var - WIN_R // 2, 0, rows - NA_SPAN)
    r = NA_ROWS * t_var[:, None] + np.arange(NA_ROWS)[None, :]
    rs = np.clip(r - WIN_R // 2, 0, rows - WIN_R)
    kr = ks[:, None] + np.arange(NA_SPAN)[None, :]
    row_ok = (kr[:, None, :] >= rs[:, :, None]) & (kr[:, None, :] < rs[:, :, None] + WIN_R)
    dr = kr[:, None, :] - r[:, :, None] + WIN_R - 1
    return np.where(row_ok, dr, -1)


def _na_bias_kernel(rpb_ref, o_ref, *, plan):
    h = pl.program_id(0)
    n_dr, n_dc = 2 * WIN_R - 1, 2 * WIN_C - 1
    qc = lax.broadcasted_iota(jnp.int32, (GRID_W, GRID_W), 0)
    kc = lax.broadcasted_iota(jnp.int32, (GRID_W, GRID_W), 1)
    cstart = jnp.clip(qc - WIN_C // 2, 0, GRID_W - WIN_C)
    col_ok = (kc >= cstart) & (kc < cstart + WIN_C)
    dc = jnp.clip(kc - qc + WIN_C - 1, 0, n_dc - 1)
    neg = jnp.full((GRID_W, GRID_W), NEG_INF, F32)
    slabs = {-1: neg}
    for dr in sorted(set(int(v) for v in plan.ravel()) - {-1}):
        acc = neg
        for j in range(n_dc):
            acc = jnp.where(col_ok & (dc == j), rpb_ref[(h * n_dr + dr) * n_dc + j], acc)
        slabs[dr] = acc
    for var in range(plan.shape[0]):
        for i in range(plan.shape[1]):
            row = jnp.concatenate([slabs[int(v)] for v in plan[var, i]], axis=1)
            o_ref[0, var, i * GRID_W:(i + 1) * GRID_W, :] = row


def _na_bias_tables(rpb, rows):
    heads = rpb.shape[0]
    shape = (heads, 3, NA_ROWS * GRID_W, NA_SPAN * GRID_W)
    return pl.pallas_call(
        functools.partial(_na_bias_kernel, plan=_na_bias_plan(rows)),
        grid=(heads,),
        in_specs=[pl.BlockSpec(memory_space=pltpu.SMEM)],
        out_specs=pl.BlockSpec((1,) + shape[1:], lambda h: (h, 0, 0, 0)),
        out_shape=jax.ShapeDtypeStruct(shape, F32),
        compiler_params=_cparams(1),
        name="na_bias_tables",
    )(rpb.astype(F32).reshape(-1))


def _na_kernel(q_ref, k_ref, v_ref, g_ref, kc_ref, vc_ref, b_ref, qw_ref, kw_ref, o_ref, kn_ref, *, rows):
    t = pl.program_id(2)
    hd = q_ref.shape[-1]

    @pl.when(t == 0)
    def _():
        kn_ref[...] = _rms(k_ref[...].astype(F32), kw_ref[...]).astype(BF16)

    qn = (_rms(q_ref[...].astype(F32), qw_ref[...]) * (hd ** -0.5)).astype(BF16)
    ks = jnp.clip(NA_ROWS * t - WIN_R // 2, 0, rows - NA_SPAN)
    start = pl.multiple_of(ks * GRID_W, GRID_W)
    k_lat = kn_ref[pl.ds(start, NA_SPAN * GRID_W), :]
    v_lat = v_ref[pl.ds(start, NA_SPAN * GRID_W), :]
    s_lat = _dot_nt(qn, k_lat) + b_ref[0, 0]
    s_ctx = _dot_nt(qn, kc_ref[0].astype(BF16))
    mx = jnp.maximum(jnp.max(s_lat, axis=-1, keepdims=True), jnp.max(s_ctx, axis=-1, keepdims=True))
    p_lat = jnp.exp(s_lat - mx)
    p_ctx = jnp.exp(s_ctx - mx)
    l = jnp.sum(p_lat, axis=-1, keepdims=True) + jnp.sum(p_ctx, axis=-1, keepdims=True)
    o = (_dot(p_lat.astype(BF16), v_lat) + _dot(p_ctx.astype(BF16), vc_ref[0].astype(BF16))) / l
    o_ref[...] = (o * _silu(g_ref[...].astype(F32))).astype(o_ref.dtype)


def _na_latent(proj, batch, seq, heads, k_ctx, v_ctx, bias, q_norm, k_norm):
    hd = LANES
    rows = seq // GRID_W
    tiles = rows // NA_ROWS
    tq = NA_ROWS * GRID_W
    lc = k_ctx.shape[1]
    qblk = lambda off: pl.BlockSpec((tq, hd), lambda b, h, t: (b * tiles + t, off * heads + h))
    full = lambda off: pl.BlockSpec((seq, hd), lambda b, h, t: (b, off * heads + h))
    ctx = pl.BlockSpec((1, lc, hd), lambda b, h, t: (b, 0, h))
    variant = lambda t: jnp.where(t == 0, 0, jnp.where(t == tiles - 1, 2, 1))
    vec = pl.BlockSpec((1, hd), lambda b, h, t: (0, 0))
    return pl.pallas_call(
        functools.partial(_na_kernel, rows=rows),
        grid=(batch, heads, tiles),
        in_specs=[
            qblk(0), full(1), full(2), qblk(3), ctx, ctx,
            pl.BlockSpec((1, 1, tq, NA_SPAN * GRID_W), lambda b, h, t: (h, variant(t), 0, 0)),
            vec, vec,
        ],
        out_specs=qblk(0),
        out_shape=jax.ShapeDtypeStruct((batch * seq, heads * hd), BF16),
        scratch_shapes=[pltpu.VMEM((seq, hd), BF16)],
        compiler_params=_cparams(3),
        name="neighbourhood_attention",
    )(proj, proj, proj, proj, k_ctx, v_ctx, bias, q_norm.reshape(1, hd), k_norm.reshape(1, hd))


def _sg_kernel(u_ref, v_ref, g_ref, nw_ref, w_ref, b_ref, o_ref):
    groups = w_ref.shape[0]
    hd = w_ref.shape[-1]
    u = _gelu(u_ref[...].astype(F32))
    v = _rms(_gelu(v_ref[...].astype(F32)), nw_ref[...]).astype(BF16)
    g = _silu(g_ref[...].astype(F32))
    for gi in range(groups):
        sl = slice(gi * hd, (gi + 1) * hd)
        mixed = _dot(w_ref[gi], v[:, sl]) + b_ref[gi]
        o_ref[:, sl] = (u[:, sl] * mixed * g[:, sl]).astype(o_ref.dtype)


def _spatial_gating(proj, width, sg_norm, sg_w, sg_b):
    m = proj.shape[0]
    groups = sg_w.shape[0]
    blk = lambda off: pl.BlockSpec((SG_CHUNK, width), lambda i: (i, off))
    bias = jnp.broadcast_to(sg_b.astype(F32)[:, :, None], (groups, SG_CHUNK, LANES))
    return pl.pallas_call(
        _sg_kernel,
        grid=(m // SG_CHUNK,),
        in_specs=[
            blk(4), blk(5), blk(6),
            pl.BlockSpec((1, width), lambda i: (0, 0)),
            pl.BlockSpec((groups, SG_CHUNK, SG_CHUNK), lambda i: (0, 0, 0)),
            pl.BlockSpec((groups, SG_CHUNK, LANES), lambda i: (0, 0, 0)),
        ],
        out_specs=pl.BlockSpec((SG_CHUNK, width), lambda i: (i, 0)),
        out_shape=jax.ShapeDtypeStruct((m, width), BF16),
        compiler_params=_cparams(1),
        name="spatial_gating",
    )(proj, proj, proj, sg_norm.reshape(1, width), sg_w.astype(BF16), bias)


def _scan_matrix(rev):
    t = np.arange(SCAN_BLOCK)
    same = (t[:, None] // SCAN_CHUNK) == (t[None, :] // SCAN_CHUNK)
    incl = same & ((t[None, :] >= t[:, None]) if rev else (t[None, :] <= t[:, None]))
    return jnp.asarray(incl, dtype=BF16)


def _chunk_rows(a, pos):
    parts = []
    for c in range(a.shape[0] // SCAN_CHUNK):
        row = a[c * SCAN_CHUNK + pos:c * SCAN_CHUNK + pos + 1]
        parts.append(jnp.broadcast_to(row, (SCAN_CHUNK, a.shape[1])))
    return jnp.concatenate(parts, axis=0)


def _hgrn_head(q, z, v, lb, tri, state, rev):
    nc = q.shape[0] // SCAN_CHUNK
    half = SCAN_CHUNK // 2
    f = lb + (1.0 - lb) * jax.nn.sigmoid(z)
    g = jnp.log(f)
    k = 1.0 - f
    g_hi = g.astype(BF16)
    r1 = g - g_hi.astype(F32)
    g_mid = r1.astype(BF16)
    g_lo = (r1 - g_mid.astype(F32)).astype(BF16)
    a = _dot(tri, g_hi) + _dot(tri, g_mid) + _dot(tri, g_lo)
    a_ref = _chunk_rows(a, half if rev else half - 1)
    a_tot = _chunk_rows(a, 0 if rev else SCAN_CHUNK - 1)
    qe = (q * jnp.exp(jnp.minimum(a - a_ref, EXP_CLAMP))).astype(BF16)
    ke = (k * jnp.exp(jnp.minimum(a_ref - a, EXP_CLAMP))).astype(BF16)
    p = jnp.where(tri > 0, _dot_nt(qe, ke), 0.0)
    o = _dot(p.astype(BF16), v)
    qa = (q * jnp.exp(a)).astype(BF16)
    kd = (k * jnp.exp(a_tot - a)).astype(BF16)
    dec = jnp.exp(a_tot)
    chunks = [slice(c * SCAN_CHUNK, (c + 1) * SCAN_CHUNK) for c in range(nc)]
    upd = [_dot_tn(v[sl], kd[sl]) for sl in chunks]
    starts = [None] * nc
    for c in (range(nc - 1, -1, -1) if rev else range(nc)):
        starts[c] = state.astype(BF16)
        state = state * dec[c * SCAN_CHUNK:c * SCAN_CHUNK + 1] + upd[c]
    inter = [_dot_nt(qa[sl], starts[c]) for c, sl in enumerate(chunks)]
    return o + jnp.concatenate(inter, axis=0), state


def _hgrn_kernel(*refs, rev, has_s0, emit_state, final):
    refs = list(refs)
    q_ref, z_ref, i_ref, lb_ref, tri_ref = refs[:5]
    del refs[:5]
    s0_ref = refs.pop(0) if has_s0 else None
    if final:
        of_ref, gate_ref, onw_ref = refs[:3]
        del refs[:3]
    o_ref = refs.pop(0)
    st_ref = refs.pop(0) if emit_state else None
    state_ref = refs.pop(0)
    t = pl.program_id(2)
    hd = LANES
    tri = tri_ref[...]

    hp = state_ref.shape[0]

    @pl.when(t == 0)
    def _():
        for hh in range(hp):
            state_ref[hh] = s0_ref[0, hh].T if has_s0 else jnp.zeros((hd, hd), F32)

    for hh in range(hp):
        sl = slice(hh * hd, (hh + 1) * hd)
        o, state = _hgrn_head(q_ref[:, sl].astype(F32), z_ref[:, sl].astype(F32), i_ref[:, sl],
                              lb_ref[:, sl], tri, state_ref[hh], rev)
        state_ref[hh] = state
        if final:
            o = _rms(o + of_ref[:, sl], onw_ref[...])
            o_ref[:, sl] = (o * _silu(gate_ref[:, sl].astype(F32))).astype(o_ref.dtype)
        else:
            o_ref[:, sl] = o

    if emit_state:
        @pl.when(t == pl.num_programs(2) - 1)
        def _():
            for hh in range(hp):
                st_ref[0, hh] = state_ref[hh].T


def _hgrn_pass(proj, batch, seq, heads, lb, s0, *, rev, emit_state, prev=None, out_norm=None):
    hd = LANES
    tb = _pick_tile(seq, SCAN_BLOCK)
    assert tb == SCAN_BLOCK
    nblk = seq // tb
    hp = SCAN_HEADS
    assert heads % hp == 0
    ng = heads // hp
    final = prev is not None
    tmap = (lambda t: nblk - 1 - t) if rev else (lambda t: t)
    blk = lambda off: pl.BlockSpec((tb, hp * hd), lambda b, h, t: (b * nblk + tmap(t), off * ng + h))
    in_specs = [
        blk(0), blk(2 if rev else 1), blk(3),
        pl.BlockSpec((1, hp * hd), lambda b, h, t: (0, h)),
        pl.BlockSpec((tb, tb), lambda b, h, t: (0, 0)),
    ]
    args = [proj, proj, proj, lb.reshape(1, heads * hd), _scan_matrix(rev)]
    if s0 is not None:
        in_specs.append(pl.BlockSpec((1, hp, hd, hd), lambda b, h, t: (b, h, 0, 0)))
        args.append(s0)
    if final:
        in_specs += [blk(0), blk(4), pl.BlockSpec((1, hd), lambda b, h, t: (0, 0))]
        args += [prev, proj, out_norm.reshape(1, hd)]
    out_specs = [blk(0)]
    out_shape = [jax.ShapeDtypeStruct((batch * seq, heads * hd), BF16 if final else F32)]
    if emit_state:
        out_specs.append(pl.BlockSpec((1, hp, hd, hd), lambda b, h, t: (b, h, 0, 0)))
        out_shape.append(jax.ShapeDtypeStruct((batch, heads, hd, hd), F32))
    res = pl.pallas_call(
        functools.partial(_hgrn_kernel, rev=rev, has_s0=s0 is not None, emit_state=emit_state, final=final),
        grid=(batch, ng, nblk),
        in_specs=in_specs,
        out_specs=out_specs,
        out_shape=out_shape,
        scratch_shapes=[pltpu.VMEM((hp, hd, hd), F32)],
        compiler_params=_cparams(3),
        name="hgrn_scan_bwd" if rev else "hgrn_scan_fwd",
    )(*args)
    return res if emit_state else (res[0], None)


_ROPE_SWAP = np.concatenate([np.arange(16, 32), np.arange(0, 16), np.arange(48, 64), np.arange(32, 48)])


def _rope_table(seq, rotate):
    if not rotate:
        row = np.concatenate([np.ones(ROPE_D), np.zeros(ROPE_D)]).astype(np.float32)
        return jnp.asarray(np.broadcast_to(row, (seq, 2 * ROPE_D)))
    t = jnp.arange(seq)
    half = ROPE_D // 4
    inv = ROPE_BASE ** (-jnp.arange(half, dtype=F32) / half)
    ang_r = (t // GRID_W).astype(F32)[:, None] * inv[None, :]
    ang_c = (t % GRID_W).astype(F32)[:, None] * inv[None, :]
    cos = jnp.concatenate([jnp.cos(ang_r)] * 2 + [jnp.cos(ang_c)] * 2, axis=-1)
    sin = jnp.concatenate([-jnp.sin(ang_r), jnp.sin(ang_r), -jnp.sin(ang_c), jnp.sin(ang_c)], axis=-1)
    return jnp.concatenate([cos, sin], axis=-1)


def _rotate_rope(r, tab, lane_lo):
    rot = r * tab
    rot = rot + pltpu.roll(rot, ROPE_D, 1)
    return jnp.where(lane_lo, rot, 0.0)


def _qup_kernel(cq_ref, nw_ref, w_ref, gn_ref, gr_ref, tab_ref, o_ref, h_ref):
    @pl.when(pl.program_id(1) == 0)
    def _():
        h_ref[...] = _rms(cq_ref[...].astype(F32), nw_ref[...]).astype(BF16)

    y = _dot(h_ref[...], w_ref[...])
    hw = NOPE_D + 2 * ROPE_D
    scale = (NOPE_D + ROPE_D) ** -0.5
    lane_lo = lax.broadcasted_iota(jnp.int32, (1, LANES), 1) < ROPE_D
    tab = tab_ref[...] * gr_ref[...]
    for hh in range(y.shape[1] // hw):
        yn = y[:, hh * hw:hh * hw + NOPE_D]
        yr = y[:, hh * hw + NOPE_D:(hh + 1) * hw]
        o_ref[:, hh * hw:hh * hw + NOPE_D] = (_rms(yn, gn_ref[...]) * scale).astype(o_ref.dtype)
        ms = jnp.sum(jnp.where(lane_lo, yr * yr, 0.0), axis=-1, keepdims=True) * (1.0 / ROPE_D)
        qr = _rotate_rope(yr, tab, lane_lo) * (lax.rsqrt(ms + EPS) * scale)
        o_ref[:, hh * hw + NOPE_D:(hh + 1) * hw] = qr.astype(o_ref.dtype)


def _q_up(proj, col_blk, q_lora, seq, a_norm, w, gain_n, gain_r, table):
    m = proj.shape[0]
    n = w.shape[1]
    tm = _pick_tile(seq, ROW_TILE)
    tn = _pick_tile(n, COL_TILE)
    nt = table.shape[0] // tm
    return pl.pallas_call(
        _qup_kernel,
        grid=(m // tm, n // tn),
        in_specs=[
            pl.BlockSpec((tm, q_lora), lambda i, j: (i, col_blk)),
            pl.BlockSpec((1, q_lora), lambda i, j: (0, 0)),
            pl.BlockSpec((q_lora, tn), lambda i, j: (0, j)),
            pl.BlockSpec((1, NOPE_D), lambda i, j: (0, 0)),
            pl.BlockSpec((1, 2 * ROPE_D), lambda i, j: (0, 0)),
            pl.BlockSpec((tm, 2 * ROPE_D), lambda i, j: (i % nt, 0)),
        ],
        out_specs=pl.BlockSpec((tm, tn), lambda i, j: (i, j)),
        out_shape=jax.ShapeDtypeStruct((m, n), BF16),
        scratch_shapes=[pltpu.VMEM((tm, q_lora), BF16)],
        compiler_params=_cparams(2),
        name="mla_q_up",
    )(proj, a_norm.reshape(1, q_lora), w, gain_n.reshape(1, NOPE_D), gain_r.reshape(1, 2 * ROPE_D), table)


def _kvup_kernel(*refs, normalize_in, emit_cache, heads):
    refs = list(refs)
    ckv_ref, kr_ref, nw_ref, gkr_ref, w_ref, gk_ref, tab_ref, k_ref, v_ref = refs[:9]
    lane_lo = lax.broadcasted_iota(jnp.int32, (1, LANES), 1) < ROPE_D
    c = ckv_ref[...].astype(F32)
    r = kr_ref[...].astype(F32)
    if normalize_in:
        c = _rms(c, nw_ref[...])
        ms = jnp.sum(jnp.where(lane_lo, r * r, 0.0), axis=-1, keepdims=True) * (1.0 / ROPE_D)
        r = r * lax.rsqrt(ms + EPS) * gkr_ref[...]
    if emit_cache:
        refs[9][...] = c
        refs[10][...] = r[:, :ROPE_D]
    k_rope = _rotate_rope(r, tab_ref[...], lane_lo).astype(k_ref.dtype)
    y = _dot(c.astype(BF16), w_ref[...])
    hw = 2 * LANES
    for h in range(heads):
        k_ref[:, h * hw:h * hw + NOPE_D] = _rms(y[:, h * hw:h * hw + NOPE_D], gk_ref[...]).astype(k_ref.dtype)
        k_ref[:, h * hw + NOPE_D:(h + 1) * hw] = k_rope
        v_ref[:, h * LANES:(h + 1) * LANES] = y[:, h * hw + NOPE_D:(h + 1) * hw].astype(v_ref.dtype)


def _kv_up(ckv_src, ckv_blk, kr_src, kr_blk, seq, kv_lora, heads, a_norm, gain_kr, w, gain_k, table,
           *, normalize_in, emit_cache):
    m = ckv_src.shape[0]
    tm = _pick_tile(seq, ROW_TILE)
    nt = table.shape[0] // tm
    out_specs = [
        pl.BlockSpec((tm, heads * 2 * LANES), lambda i: (i, 0)),
        pl.BlockSpec((tm, heads * LANES), lambda i: (i, 0)),
    ]
    out_shape = [
        jax.ShapeDtypeStruct((m, heads * 2 * LANES), BF16),
        jax.ShapeDtypeStruct((m, heads * LANES), BF16),
    ]
    if emit_cache:
        out_specs += [pl.BlockSpec((tm, kv_lora), lambda i: (i, 0)), pl.BlockSpec((tm, ROPE_D), lambda i: (i, 0))]
        out_shape += [jax.ShapeDtypeStruct((m, kv_lora), F32), jax.ShapeDtypeStruct((m, ROPE_D), F32)]
    return pl.pallas_call(
        functools.partial(_kvup_kernel, normalize_in=normalize_in, emit_cache=emit_cache, heads=heads),
        grid=(m // tm,),
        in_specs=[
            pl.BlockSpec((tm, kv_lora), lambda i: (i, ckv_blk)),
            pl.BlockSpec((tm, 2 * ROPE_D), lambda i: (i, kr_blk)),
            pl.BlockSpec((1, kv_lora), lambda i: (0, 0)),
            pl.BlockSpec((1, 2 * ROPE_D), lambda i: (0, 0)),
            pl.BlockSpec((kv_lora, heads * 2 * LANES), lambda i: (0, 0)),
            pl.BlockSpec((1, NOPE_D), lambda i: (0, 0)),
            pl.BlockSpec((tm, 2 * ROPE_D), lambda i: (i % nt, 0)),
        ],
        out_specs=out_specs,
        out_shape=out_shape,
        compiler_params=_cparams(1),
        name="mla_kv_up",
    )(ckv_src, kr_src, a_norm.reshape(1, kv_lora), gain_kr.reshape(1, 2 * ROPE_D), w,
      gain_k.reshape(1, NOPE_D), table)


def _mla_attn_kernel(*refs, has_ctx):
    if has_ctx:
        q_ref, k_ref, v_ref, kc_ref, vc_ref, g_ref, o_ref = refs
    else:
        q_ref, k_ref, v_ref, g_ref, o_ref = refs
    q = q_ref[...]
    s = _dot_nt(q, k_ref[...])
    mx = jnp.max(s, axis=-1, keepdims=True)
    if has_ctx:
        s_c = _dot_nt(q, kc_ref[...])
        mx = jnp.maximum(mx, jnp.max(s_c, axis=-1, keepdims=True))
    p = jnp.exp(s - mx)
    l = jnp.sum(p, axis=-1, keepdims=True)
    acc = _dot(p.astype(BF16), v_ref[...])
    if has_ctx:
        p_c = jnp.exp(s_c - mx)
        l = l + jnp.sum(p_c, axis=-1, keepdims=True)
        acc = acc + _dot(p_c.astype(BF16), vc_ref[...])
    o_ref[...] = (acc / l * _silu(g_ref[...].astype(F32))).astype(o_ref.dtype)


def _mla_attention(q, k, v, proj, gate_blk, batch, seq, heads, ctx=None):
    tq = _pick_tile(seq, ATTN_Q)
    nq = seq // tq
    qmap = lambda b, h, t: (b * nq + t, h)
    in_specs = [
        pl.BlockSpec((tq, 2 * LANES), qmap),
        pl.BlockSpec((seq, 2 * LANES), lambda b, h, t: (b, h)),
        pl.BlockSpec((seq, LANES), lambda b, h, t: (b, h)),
    ]
    args = [q, k, v]
    if ctx is not None:
        k_c, v_c = ctx
        lc = k_c.shape[0] // batch
        in_specs += [
            pl.BlockSpec((lc, 2 * LANES), lambda b, h, t: (b, h)),
            pl.BlockSpec((lc, LANES), lambda b, h, t: (b, h)),
        ]
        args += [k_c, v_c]
    in_specs.append(pl.BlockSpec((tq, LANES), lambda b, h, t: (b * nq + t, gate_blk + h)))
    args.append(proj)
    return pl.pallas_call(
        functools.partial(_mla_attn_kernel, has_ctx=ctx is not None),
        grid=(batch, heads, nq),
        in_specs=in_specs,
        out_specs=pl.BlockSpec((tq, LANES), qmap),
        out_shape=jax.ShapeDtypeStruct((batch * seq, heads * LANES), BF16),
        compiler_params=_cparams(3),
        name="mla_attention",
    )(*args)


def _pad_cols(w, mult):
    pad = (-w.shape[1]) % mult
    return w if pad == 0 else jnp.concatenate([w, jnp.zeros((w.shape[0], pad), w.dtype)], axis=1)


def kernel(x_prompt, x_sample, cache_na_k, cache_na_v, state_hgrn, cache_mla_ckv, cache_mla_krope, c, c_ctx,
           norm_w, w_ada, b_ada, w_out,
           w_in_ab, na_q_norm, na_k_norm, na_rpb, sg_norm, sg_w, sg_b,
           w_in_cd, hgrn_lb, hgrn_out_norm, mla_q_a_norm, mla_w_q_up, mla_kv_a_norm, mla_w_kv_up,
           mla_q_norm, mla_k_norm):
    bp, lp, d = x_prompt.shape
    bs, ls, _ = x_sample.shape
    depth = norm_w.shape[0]
    hd = cache_na_k.shape[-1]
    heads = cache_na_k.shape[-2]
    width = heads * hd
    assert hd == LANES and state_hgrn.shape[-2:] == (hd, hd) and ls % GRID_W == 0
    q_lora = mla_q_a_norm.shape[-1]
    kv_lora = mla_kv_a_norm.shape[-1]
    assert mla_q_norm.shape[-1] == NOPE_D + ROPE_D and cache_mla_krope.shape[-1] == ROPE_D
    assert mla_w_kv_up.shape[-1] == heads * 2 * LANES

    n_cond = bs + 1
    cond_rows = -(-n_cond // 8) * 8
    cond = jnp.concatenate([c, c_ctx[None, :], jnp.zeros((cond_rows - n_cond, d), F32)], axis=0)
    mod = _modulation(cond, w_ada, b_ada).reshape(depth, cond_rows, 3, 1, d)

    lb_soft = jax.nn.softmax(hgrn_lb.astype(F32), axis=0)
    lb_cum = jnp.cumsum(lb_soft, axis=0)
    lower_bounds = lb_cum - lb_cum[:1]

    xp = x_prompt.reshape(bp * lp, d)
    xs = x_sample.reshape(bs * ls, d)
    na_k_new, na_v_new, hgrn_new, ckv_new, kr_new = [], [], [], [], []
    for layer in range(depth):
        j = layer // 2
        shift_s, scale_s, gate_s = (mod[layer, :bs, i] for i in range(3))
        shift_p, scale_p, gate_p = (mod[layer, bs:bs + 1, i] for i in range(3))
        w_o = w_out[layer].astype(BF16)
        if layer % 2 == 0:
            w_in = _pad_cols(w_in_ab[j].astype(BF16), COL_TILE)
            proj_p = _in_proj(xp, lp, norm_w[layer], scale_p, shift_p, w_in)
            proj_s = _in_proj(xs, ls, norm_w[layer], scale_s, shift_s, w_in)
            mix_a_p, k_new, v_new = _attn_ctx0(proj_p, bp, lp, heads, na_q_norm[j], na_k_norm[j])
            na_k_new.append(k_new.reshape(bp, lp, heads, hd))
            na_v_new.append(v_new.reshape(bp, lp, heads, hd))
            bias = _na_bias_tables(na_rpb[j], ls // GRID_W)
            lc = cache_na_k.shape[2]
            mix_a_s = _na_latent(proj_s, bs, ls, heads,
                                 cache_na_k[:, j].reshape(bs, lc, width), cache_na_v[:, j].reshape(bs, lc, width),
                                 bias, na_q_norm[j], na_k_norm[j])
            mix_b_p = _spatial_gating(proj_p, width, sg_norm[j], sg_w[j], sg_b[j])
            mix_b_s = _spatial_gating(proj_s, width, sg_norm[j], sg_w[j], sg_b[j])
        else:
            w_cd = w_in_cd[j]
            kr_off = 5 * width + q_lora + kv_lora
            w_kr = w_cd[:, kr_off:kr_off + ROPE_D]
            w_in = jnp.concatenate([w_cd[:, :kr_off + ROPE_D], w_kr[:, _ROPE_SWAP], w_cd[:, kr_off + ROPE_D:]], axis=1)
            w_in = _pad_cols(w_in.astype(BF16), COL_TILE)
            assert (5 * width) % q_lora == 0 and (5 * width + q_lora) % kv_lora == 0
            cq_blk = (5 * width) // q_lora
            ckv_blk = (5 * width + q_lora) // kv_lora
            kr_blk = kr_off // LANES
            gd_blk = kr_blk + 1
            proj_p = _in_proj(xp, lp, norm_w[layer], scale_p, shift_p, w_in)
            proj_s = _in_proj(xs, ls, norm_w[layer], scale_s, shift_s, w_in)

            lb_f, lb_b = lower_bounds[layer, 0], lower_bounds[layer, 1]
            of_p, st_f = _hgrn_pass(proj_p, bp, lp, heads, lb_f, None, rev=False, emit_state=True)
            mix_a_p, st_b = _hgrn_pass(proj_p, bp, lp, heads, lb_b, None, rev=True, emit_state=True,
                                       prev=of_p, out_norm=hgrn_out_norm[j])
            hgrn_new.append(jnp.stack([st_f, st_b], axis=1))
            of_s, _ = _hgrn_pass(proj_s, bs, ls, heads, lb_f, state_hgrn[:, j, 0], rev=False, emit_state=False)
            mix_a_s, _ = _hgrn_pass(proj_s, bs, ls, heads, lb_b, state_hgrn[:, j, 1], rev=True, emit_state=False,
                                    prev=of_s, out_norm=hgrn_out_norm[j])

            wq = mla_w_q_up[j].reshape(q_lora, heads, NOPE_D + ROPE_D)
            wq = jnp.concatenate([wq, wq[:, :, NOPE_D:][:, :, _ROPE_SWAP]], axis=-1)
            wq = wq.reshape(q_lora, heads * 2 * LANES).astype(BF16)
            wkv = mla_w_kv_up[j].astype(BF16)
            gq, gk = mla_q_norm[j], mla_k_norm[j]
            gq_r = jnp.concatenate([gq[NOPE_D:], gq[NOPE_D:][_ROPE_SWAP]])
            gk_r = jnp.concatenate([gk[NOPE_D:], gk[NOPE_D:][_ROPE_SWAP]])
            tab_p = _rope_table(_pick_tile(lp, ROW_TILE), False)
            tab_s = _rope_table(ls, True)
            tab_c = _rope_table(_pick_tile(cache_mla_ckv.shape[2], ROW_TILE), False)

            q_p = _q_up(proj_p, cq_blk, q_lora, lp, mla_q_a_norm[j], wq, gq[:NOPE_D], gq_r, tab_p)
            k_p, v_p, ckv_n, kr_n = _kv_up(proj_p, ckv_blk, proj_p, kr_blk, lp, kv_lora, heads, mla_kv_a_norm[j],
                                           gk_r, wkv, gk[:NOPE_D], tab_p, normalize_in=True, emit_cache=True)
            ckv_new.append(ckv_n.reshape(bp, lp, kv_lora))
            kr_new.append(kr_n.reshape(bp, lp, ROPE_D))
            mix_b_p = _mla_attention(q_p, k_p, v_p, proj_p, gd_blk, bp, lp, heads)

            q_s = _q_up(proj_s, cq_blk, q_lora, ls, mla_q_a_norm[j], wq, gq[:NOPE_D], gq_r, tab_s)
            k_s, v_s = _kv_up(proj_s, ckv_blk, proj_s, kr_blk, ls, kv_lora, heads, mla_kv_a_norm[j],
                              gk_r, wkv, gk[:NOPE_D], tab_s, normalize_in=True, emit_cache=False)
            lc = cache_mla_ckv.shape[2]
            kr_c = cache_mla_krope[:, j].reshape(bs * lc, ROPE_D)
            kr_c = jnp.concatenate([kr_c, jnp.zeros_like(kr_c)], axis=1)
            k_c, v_c = _kv_up(cache_mla_ckv[:, j].reshape(bs * lc, kv_lora), 0, kr_c, 0, lc, kv_lora, heads,
                              mla_kv_a_norm[j], gk_r, wkv, gk[:NOPE_D], tab_c, normalize_in=False, emit_cache=False)
            mix_b_s = _mla_attention(q_s, k_s, v_s, proj_s, gd_blk, bs, ls, heads, ctx=(k_c, v_c))
        xp = _out_proj(mix_a_p, mix_b_p, w_o, xp, lp, gate_p)
        xs = _out_proj(mix_a_s, mix_b_s, w_o, xs, ls, gate_s)
    return (xp.reshape(bp, lp, d), xs.reshape(bs, ls, d),
            jnp.stack(na_k_new, axis=1), jnp.stack(na_v_new, axis=1), jnp.stack(hgrn_new, axis=1),
            jnp.stack(ckv_new, axis=1), jnp.stack(kr_new, axis=1))
```

```python
import functools

import numpy as np
import jax
import jax.numpy as jnp
from jax import lax
from jax.experimental import pallas as pl
from jax.experimental.pallas import tpu as pltpu

F32 = jnp.float32
BF16 = jnp.bfloat16

GRID_W = 64
WIN_R = 8
WIN_C = 16
SG_CHUNK = 128
NOPE_D = 128
ROPE_D = 64
ROPE_BASE = 10000.0
EPS = 1e-6
NEG_INF = -1e30

LANES = 128
VMEM_LIMIT = 56 * 1024 * 1024

ROW_TILE = 512
COL_TILE = 1024
NA_ROWS = 4
NA_SPAN = NA_ROWS + WIN_R
ATTN_Q = 256
SCAN_BLOCK = 256
SCAN_CHUNK = 32
SCAN_HEADS = 8
CTX_ATTN_HEADS = 4
LAT_ATTN_HEADS = 2
LOG2E = 1.4426950408889634
EXP_CLAMP = 80.0


def _cparams(n_axes):
    return pltpu.CompilerParams(
        dimension_semantics=("arbitrary",) * n_axes, vmem_limit_bytes=VMEM_LIMIT)


def _pick_tile(n, pref):
    t = pref
    while n % t:
        t //= 2
    assert t >= LANES or t == n, (n, pref)
    return t


def _rms(x, w):
    return x * lax.rsqrt(jnp.mean(x * x, axis=-1, keepdims=True) + EPS) * w


def _silu(x):
    return x * jax.nn.sigmoid(x)


def _gelu(x):
    return 0.5 * x * (1.0 + jnp.tanh(0.7978845608028654 * (x + 0.044715 * (x * x * x))))


def _dot(a, b):
    return jnp.dot(a, b, preferred_element_type=F32)


def _dot_nt(a, b):
    return lax.dot_general(a, b, (((1,), (1,)), ((), ())), preferred_element_type=F32)


def _dot_tn(a, b):
    return lax.dot_general(a, b, (((0,), (0,)), ((), ())), preferred_element_type=F32)


def _mod_kernel(c_ref, w_ref, b_ref, o_ref):
    a = _silu(c_ref[...]).astype(BF16)
    o_ref[0] = _dot(a, w_ref[0].astype(BF16)) + b_ref[0]


def _modulation(cond, w_ada, b_ada):
    depth, d, n = w_ada.shape
    r = cond.shape[0]
    tn = _pick_tile(n, 512)
    return pl.pallas_call(
        _mod_kernel,
        grid=(depth, n // tn),
        in_specs=[
            pl.BlockSpec((r, d), lambda l, j: (0, 0)),
            pl.BlockSpec((1, d, tn), lambda l, j: (l, 0, j)),
            pl.BlockSpec((1, 1, tn), lambda l, j: (l, 0, j)),
        ],
        out_specs=pl.BlockSpec((1, r, tn), lambda l, j: (l, 0, j)),
        out_shape=jax.ShapeDtypeStruct((depth, r, n), F32),
        compiler_params=_cparams(2),
        name="adaln_modulation",
    )(cond, w_ada, b_ada.reshape(depth, 1, n))


def _in_kernel(x_ref, nw_ref, sc_ref, sh_ref, w_ref, o_ref, h_ref):
    @pl.when(pl.program_id(1) == 0)
    def _():
        h = _rms(x_ref[...], nw_ref[...])
        h_ref[...] = (h * (1.0 + sc_ref[0]) + sh_ref[0]).astype(BF16)

    o_ref[...] = _dot(h_ref[...], w_ref[...]).astype(o_ref.dtype)


def _in_proj(x2, seq, norm_w, scale, shift, w):
    m, d = x2.shape
    n = w.shape[1]
    nb = scale.shape[0]
    tm = _pick_tile(m, ROW_TILE)
    tn = _pick_tile(n, COL_TILE)
    if nb == 1:
        cond_map = lambda i, j: (0, 0, 0)
    else:
        assert seq % tm == 0
        cond_map = lambda i, j: ((i * tm) // seq, 0, 0)
    return pl.pallas_call(
        _in_kernel,
        grid=(m // tm, n // tn),
        in_specs=[
            pl.BlockSpec((tm, d), lambda i, j: (i, 0)),
            pl.BlockSpec((1, d), lambda i, j: (0, 0)),
            pl.BlockSpec((1, 1, d), cond_map),
            pl.BlockSpec((1, 1, d), cond_map),
            pl.BlockSpec((d, tn), lambda i, j: (0, j)),
        ],
        out_specs=pl.BlockSpec((tm, tn), lambda i, j: (i, j)),
        out_shape=jax.ShapeDtypeStruct((m, n), BF16),
        scratch_shapes=[pltpu.VMEM((tm, d), BF16)],
        compiler_params=_cparams(2),
        name="norm_mod_in_proj",
    )(x2, norm_w.reshape(1, d), scale, shift, w)


def _out_kernel(a_ref, b_ref, wa_ref, wb_ref, x_ref, g_ref, o_ref):
    acc = _dot(a_ref[...], wa_ref[...]) + _dot(b_ref[...], wb_ref[...])
    o_ref[...] = x_ref[...] + g_ref[0] * acc


def _out_proj(mix_a, mix_b, w_out, x2, seq, gate):
    m, d = x2.shape
    wdt = mix_a.shape[1]
    nb = gate.shape[0]
    tm = _pick_tile(m, ROW_TILE)
    tn = _pick_tile(d, COL_TILE)
    if nb == 1:
        cond_map = lambda i, j: (0, 0, j)
    else:
        assert seq % tm == 0
        cond_map = lambda i, j: ((i * tm) // seq, 0, j)
    return pl.pallas_call(
        _out_kernel,
        grid=(m // tm, d // tn),
        in_specs=[
            pl.BlockSpec((tm, wdt), lambda i, j: (i, 0)),
            pl.BlockSpec((tm, wdt), lambda i, j: (i, 0)),
            pl.BlockSpec((wdt, tn), lambda i, j: (0, j)),
            pl.BlockSpec((wdt, tn), lambda i, j: (1, j)),
            pl.BlockSpec((tm, tn), lambda i, j: (i, j)),
            pl.BlockSpec((1, 1, tn), cond_map),
        ],
        out_specs=pl.BlockSpec((tm, tn), lambda i, j: (i, j)),
        out_shape=jax.ShapeDtypeStruct((m, d), F32),
        compiler_params=_cparams(2),
        name="out_proj_residual",
    )(mix_a, mix_b, w_out, w_out, x2, gate)


def _attn_ctx0_kernel(q_ref, k_ref, v_ref, g_ref, qw_ref, kw_ref, o_ref, ko_ref, vo_ref):
    hd = LANES
    qscale = hd ** -0.5 * LOG2E
    scores = []
    for hh in range(q_ref.shape[-1] // hd):
        sl = slice(hh * hd, (hh + 1) * hd)
        qn = _rms(q_ref[:, sl].astype(F32), qw_ref[...]) * qscale
        kn = _rms(k_ref[:, sl].astype(F32), kw_ref[...])
        ko_ref[:, sl] = kn
        vo_ref[:, sl] = v_ref[:, sl].astype(F32)
        scores.append(_dot_nt(qn.astype(BF16), kn.astype(BF16)))
    for hh, s in enumerate(scores):
        sl = slice(hh * hd, (hh + 1) * hd)
        p = jnp.exp2(s - jnp.max(s, axis=-1, keepdims=True))
        l = jnp.sum(p, axis=-1, keepdims=True)
        o = _dot(p.astype(BF16), v_ref[:, sl]) / l
        o_ref[:, sl] = (o * _silu(g_ref[:, sl].astype(F32))).astype(o_ref.dtype)


def _attn_ctx0(proj, batch, seq, heads, q_norm, k_norm):
    hd = LANES
    m = batch * seq
    hp = CTX_ATTN_HEADS
    assert heads % hp == 0
    ng = heads // hp
    blk = lambda off: pl.BlockSpec((seq, hp * hd), lambda b, h: (b, off * ng + h))
    vec = pl.BlockSpec((1, hd), lambda b, h: (0, 0))
    return pl.pallas_call(
        _attn_ctx0_kernel,
        grid=(batch, ng),
        in_specs=[blk(0), blk(1), blk(2), blk(3), vec, vec],
        out_specs=[blk(0), blk(0), blk(0)],
        out_shape=[
            jax.ShapeDtypeStruct((m, heads * hd), BF16),
            jax.ShapeDtypeStruct((m, heads * hd), F32),
            jax.ShapeDtypeStruct((m, heads * hd), F32),
        ],
        compiler_params=_cparams(2),
        name="ctx_attention_l0",
    )(proj, proj, proj, proj, q_norm.reshape(1, hd), k_norm.reshape(1, hd))


def _na_bias_plan(rows):
    tiles = rows // NA_ROWS
    assert min(WIN_R, rows) == WIN_R and tiles >= 3
    t_var = np.array([0, 1, tiles - 1])
    ks = np.clip(NA_ROWS * t_var - WIN_R // 2, 0, rows - NA_SPAN)
    r = NA_ROWS * t_var[:, None] + np.arange(NA_ROWS)[None, :]
    rs = np.clip(r - WIN_R // 2, 0, rows - WIN_R)
    kr = ks[:, None] + np.arange(NA_SPAN)[None, :]
    row_ok = (kr[:, None, :] >= rs[:, :, None]) & (kr[:, None, :] < rs[:, :, None] + WIN_R)
    dr = kr[:, None, :] - r[:, :, None] + WIN_R - 1
    return np.where(row_ok, dr, -1)


def _na_bias_kernel(rpb_ref, o_ref, *, plan):
    h = pl.program_id(0)
    n_dr, n_dc = 2 * WIN_R - 1, 2 * WIN_C - 1
    qc = lax.broadcasted_iota(jnp.int32, (GRID_W, GRID_W), 0)
    kc = lax.broadcasted_iota(jnp.int32, (GRID_W, GRID_W), 1)
    cstart = jnp.clip(qc - WIN_C // 2, 0, GRID_W - WIN_C)
    col_ok = (kc >= cstart) & (kc < cstart + WIN_C)
    dc = jnp.clip(kc - qc + WIN_C - 1, 0, n_dc - 1)
    neg = jnp.full((GRID_W, GRID_W), NEG_INF, F32)
    slabs = {-1: neg}
    for dr in sorted(set(int(v) for v in plan.ravel()) - {-1}):
        acc = neg
        for j in range(n_dc):
            acc = jnp.where(col_ok & (dc == j), rpb_ref[(h * n_dr + dr) * n_dc + j] * LOG2E, acc)
        slabs[dr] = acc
    for var in range(plan.shape[0]):
        for i in range(plan.shape[1]):
            row = jnp.concatenate([slabs[int(v)] for v in plan[var, i]], axis=1)
            o_ref[0, var, i * GRID_W:(i + 1) * GRID_W, :] = row


def _na_bias_tables(rpb, rows):
    heads = rpb.shape[0]
    shape = (heads, 3, NA_ROWS * GRID_W, NA_SPAN * GRID_W)
    return pl.pallas_call(
        functools.partial(_na_bias_kernel, plan=_na_bias_plan(rows)),
        grid=(heads,),
        in_specs=[pl.BlockSpec(memory_space=pltpu.SMEM)],
        out_specs=pl.BlockSpec((1,) + shape[1:], lambda h: (h, 0, 0, 0)),
        out_shape=jax.ShapeDtypeStruct(shape, F32),
        compiler_params=_cparams(1),
        name="na_bias_tables",
    )(rpb.astype(F32).reshape(-1))


def _na_kernel(q_ref, k_ref, v_ref, g_ref, kc_ref, vc_ref, b_ref, qw_ref, kw_ref, o_ref, kn_ref, *, rows):
    t = pl.program_id(2)
    hd = LANES
    hp = q_ref.shape[-1] // hd

    @pl.when(t == 0)
    def _():
        for hh in range(hp):
            sl = slice(hh * hd, (hh + 1) * hd)
            kn_ref[:, sl] = _rms(k_ref[:, sl].astype(F32), kw_ref[...]).astype(BF16)

    ks = jnp.clip(NA_ROWS * t - WIN_R // 2, 0, rows - NA_SPAN)
    span = pl.ds(pl.multiple_of(ks * GRID_W, GRID_W), NA_SPAN * GRID_W)
    qscale = hd ** -0.5 * LOG2E
    scores = []
    for hh in range(hp):
        sl = slice(hh * hd, (hh + 1) * hd)
        qn = (_rms(q_ref[:, sl].astype(F32), qw_ref[...]) * qscale).astype(BF16)
        scores.append((_dot_nt(qn, kn_ref[span, sl]) + b_ref[hh, 0], _dot_nt(qn, kc_ref[0, :, sl])))
    for hh, (s_lat, s_ctx) in enumerate(scores):
        sl = slice(hh * hd, (hh + 1) * hd)
        mx = jnp.maximum(jnp.max(s_lat, axis=-1, keepdims=True), jnp.max(s_ctx, axis=-1, keepdims=True))
        p_lat = jnp.exp2(s_lat - mx)
        p_ctx = jnp.exp2(s_ctx - mx)
        l = jnp.sum(p_lat, axis=-1, keepdims=True) + jnp.sum(p_ctx, axis=-1, keepdims=True)
        o = (_dot(p_lat.astype(BF16), v_ref[span, sl]) + _dot(p_ctx.astype(BF16), vc_ref[0, :, sl])) / l
        o_ref[:, sl] = (o * _silu(g_ref[:, sl].astype(F32))).astype(o_ref.dtype)


def _na_latent(proj, batch, seq, heads, k_ctx, v_ctx, bias, q_norm, k_norm):
    hd = LANES
    rows = seq // GRID_W
    tiles = rows // NA_ROWS
    tq = NA_ROWS * GRID_W
    lc = k_ctx.shape[1]
    hp = LAT_ATTN_HEADS
    assert heads % hp == 0
    ng = heads // hp
    qblk = lambda off: pl.BlockSpec((tq, hp * hd), lambda b, h, t: (b * tiles + t, off * ng + h))
    full = lambda off: pl.BlockSpec((seq, hp * hd), lambda b, h, t: (b, off * ng + h))
    ctx = pl.BlockSpec((1, lc, hp * hd), lambda b, h, t: (b, 0, h))
    variant = lambda t: jnp.where(t == 0, 0, jnp.where(t == tiles - 1, 2, 1))
    vec = pl.BlockSpec((1, hd), lambda b, h, t: (0, 0))
    return pl.pallas_call(
        functools.partial(_na_kernel, rows=rows),
        grid=(batch, ng, tiles),
        in_specs=[
            qblk(0), full(1), full(2), qblk(3), ctx, ctx,
            pl.BlockSpec((hp, 1, tq, NA_SPAN * GRID_W), lambda b, h, t: (h, variant(t), 0, 0)),
            vec, vec,
        ],
        out_specs=qblk(0),
        out_shape=jax.ShapeDtypeStruct((batch * seq, heads * hd), BF16),
        scratch_shapes=[pltpu.VMEM((seq, hp * hd), BF16)],
        compiler_params=_cparams(3),
        name="neighbourhood_attention",
    )(proj, proj, proj, proj, k_ctx, v_ctx, bias, q_norm.reshape(1, hd), k_norm.reshape(1, hd))


def _sg_kernel(u_ref, v_ref, g_ref, nw_ref, w_ref, b_ref, o_ref):
    groups = w_ref.shape[0]
    hd = w_ref.shape[-1]
    u = _gelu(u_ref[...].astype(F32))
    v = _rms(_gelu(v_ref[...].astype(F32)), nw_ref[...]).astype(BF16)
    g = _silu(g_ref[...].astype(F32))
    for gi in range(groups):
        sl = slice(gi * hd, (gi + 1) * hd)
        mixed = _dot(w_ref[gi], v[:, sl]) + b_ref[gi]
        o_ref[:, sl] = (u[:, sl] * mixed * g[:, sl]).astype(o_ref.dtype)


def _spatial_gating(proj, width, sg_norm, sg_w, sg_b):
    m = proj.shape[0]
    groups = sg_w.shape[0]
    blk = lambda off: pl.BlockSpec((SG_CHUNK, width), lambda i: (i, off))
    bias = jnp.broadcast_to(sg_b.astype(F32)[:, :, None], (groups, SG_CHUNK, LANES))
    return pl.pallas_call(
        _sg_kernel,
        grid=(m // SG_CHUNK,),
        in_specs=[
            blk(4), blk(5), blk(6),
            pl.BlockSpec((1, width), lambda i: (0, 0)),
            pl.BlockSpec((groups, SG_CHUNK, SG_CHUNK), lambda i: (0, 0, 0)),
            pl.BlockSpec((groups, SG_CHUNK, LANES), lambda i: (0, 0, 0)),
        ],
        out_specs=pl.BlockSpec((SG_CHUNK, width), lambda i: (i, 0)),
        out_shape=jax.ShapeDtypeStruct((m, width), BF16),
        compiler_params=_cparams(1),
        name="spatial_gating",
    )(proj, proj, proj, sg_norm.reshape(1, width), sg_w.astype(BF16), bias)


def _scan_matrix(rev):
    t = np.arange(SCAN_BLOCK)
    same = (t[:, None] // SCAN_CHUNK) == (t[None, :] // SCAN_CHUNK)
    incl = same & ((t[None, :] >= t[:, None]) if rev else (t[None, :] <= t[:, None]))
    return jnp.asarray(incl, dtype=BF16)


def _chunk_rows(a, pos):
    parts = []
    for c in range(a.shape[0] // SCAN_CHUNK):
        row = a[c * SCAN_CHUNK + pos:c * SCAN_CHUNK + pos + 1]
        parts.append(jnp.broadcast_to(row, (SCAN_CHUNK, a.shape[1])))
    return jnp.concatenate(parts, axis=0)


def _hgrn_head(q, z, v, lb, tri, state, rev):
    nc = q.shape[0] // SCAN_CHUNK
    half = SCAN_CHUNK // 2
    f = lb + (1.0 - lb) * jax.nn.sigmoid(z)
    g = jnp.log(f)
    k = 1.0 - f
    g_hi = g.astype(BF16)
    r1 = g - g_hi.astype(F32)
    g_mid = r1.astype(BF16)
    g_lo = (r1 - g_mid.astype(F32)).astype(BF16)
    a = _dot(tri, g_hi) + _dot(tri, g_mid) + _dot(tri, g_lo)
    a_ref = _chunk_rows(a, half if rev else half - 1)
    a_tot = _chunk_rows(a, 0 if rev else SCAN_CHUNK - 1)
    qe = (q * jnp.exp(jnp.minimum(a - a_ref, EXP_CLAMP))).astype(BF16)
    ke = (k * jnp.exp(jnp.minimum(a_ref - a, EXP_CLAMP))).astype(BF16)
    p = jnp.where(tri > 0, _dot_nt(qe, ke), 0.0)
    o = _dot(p.astype(BF16), v)
    qa = (q * jnp.exp(a)).astype(BF16)
    kd = (k * jnp.exp(a_tot - a)).astype(BF16)
    dec = jnp.exp(a_tot)
    chunks = [slice(c * SCAN_CHUNK, (c + 1) * SCAN_CHUNK) for c in range(nc)]
    upd = [_dot_tn(v[sl], kd[sl]) for sl in chunks]
    starts = [None] * nc
    for c in (range(nc - 1, -1, -1) if rev else range(nc)):
        starts[c] = state.astype(BF16)
        state = state * dec[c * SCAN_CHUNK:c * SCAN_CHUNK + 1] + upd[c]
    inter = [_dot_nt(qa[sl], starts[c]) for c, sl in enumerate(chunks)]
    return o + jnp.concatenate(inter, axis=0), state


def _hgrn_kernel(*refs, rev, has_s0, emit_state, final):
    refs = list(refs)
    q_ref, z_ref, i_ref, lb_ref, tri_ref = refs[:5]
    del refs[:5]
    s0_ref = refs.pop(0) if has_s0 else None
    if final:
        of_ref, gate_ref, onw_ref = refs[:3]
        del refs[:3]
    o_ref = refs.pop(0)
    st_ref = refs.pop(0) if emit_state else None
    state_ref = refs.pop(0)
    t = pl.program_id(2)
    hd = LANES
    tri = tri_ref[...]

    hp = state_ref.shape[0]

    @pl.when(t == 0)
    def _():
        for hh in range(hp):
            state_ref[hh] = s0_ref[0, hh].T if has_s0 else jnp.zeros((hd, hd), F32)

    for hh in range(hp):
        sl = slice(hh * hd, (hh + 1) * hd)
        o, state = _hgrn_head(q_ref[:, sl].astype(F32), z_ref[:, sl].astype(F32), i_ref[:, sl],
                              lb_ref[:, sl], tri, state_ref[hh], rev)
        state_ref[hh] = state
        if final:
            o = _rms(o + of_ref[:, sl], onw_ref[...])
            o_ref[:, sl] = (o * _silu(gate_ref[:, sl].astype(F32))).astype(o_ref.dtype)
        else:
            o_ref[:, sl] = o

    if emit_state:
        @pl.when(t == pl.num_programs(2) - 1)
        def _():
            for hh in range(hp):
                st_ref[0, hh] = state_ref[hh].T


def _hgrn_pass(proj, batch, seq, heads, lb, s0, *, rev, emit_state, prev=None, out_norm=None):
    hd = LANES
    tb = _pick_tile(seq, SCAN_BLOCK)
    assert tb == SCAN_BLOCK
    nblk = seq // tb
    hp = SCAN_HEADS
    assert heads % hp == 0
    ng = heads // hp
    final = prev is not None
    tmap = (lambda t: nblk - 1 - t) if rev else (lambda t: t)
    blk = lambda off: pl.BlockSpec((tb, hp * hd), lambda b, h, t: (b * nblk + tmap(t), off * ng + h))
    in_specs = [
        blk(0), blk(2 if rev else 1), blk(3),
        pl.BlockSpec((1, hp * hd), lambda b, h, t: (0, h)),
        pl.BlockSpec((tb, tb), lambda b, h, t: (0, 0)),
    ]
    args = [proj, proj, proj, lb.reshape(1, heads * hd), _scan_matrix(rev)]
    if s0 is not None:
        in_specs.append(pl.BlockSpec((1, hp, hd, hd), lambda b, h, t: (b, h, 0, 0)))
        args.append(s0)
    if final:
        in_specs += [blk(0), blk(4), pl.BlockSpec((1, hd), lambda b, h, t: (0, 0))]
        args += [prev, proj, out_norm.reshape(1, hd)]
    out_specs = [blk(0)]
    out_shape = [jax.ShapeDtypeStruct((batch * seq, heads * hd), BF16 if final else F32)]
    if emit_state:
        out_specs.append(pl.BlockSpec((1, hp, hd, hd), lambda b, h, t: (b, h, 0, 0)))
        out_shape.append(jax.ShapeDtypeStruct((batch, heads, hd, hd), F32))
    res = pl.pallas_call(
        functools.partial(_hgrn_kernel, rev=rev, has_s0=s0 is not None, emit_state=emit_state, final=final),
        grid=(batch, ng, nblk),
        in_specs=in_specs,
        out_specs=out_specs,
        out_shape=out_shape,
        scratch_shapes=[pltpu.VMEM((hp, hd, hd), F32)],
        compiler_params=_cparams(3),
        name="hgrn_scan_bwd" if rev else "hgrn_scan_fwd",
    )(*args)
    return res if emit_state else (res[0], None)


_ROPE_SWAP = np.concatenate([np.arange(16, 32), np.arange(0, 16), np.arange(48, 64), np.arange(32, 48)])


def _rope_table(seq, rotate):
    if not rotate:
        row = np.concatenate([np.ones(ROPE_D), np.zeros(ROPE_D)]).astype(np.float32)
        return jnp.asarray(np.broadcast_to(row, (seq, 2 * ROPE_D)))
    t = jnp.arange(seq)
    half = ROPE_D // 4
    inv = ROPE_BASE ** (-jnp.arange(half, dtype=F32) / half)
    ang_r = (t // GRID_W).astype(F32)[:, None] * inv[None, :]
    ang_c = (t % GRID_W).astype(F32)[:, None] * inv[None, :]
    cos = jnp.concatenate([jnp.cos(ang_r)] * 2 + [jnp.cos(ang_c)] * 2, axis=-1)
    sin = jnp.concatenate([-jnp.sin(ang_r), jnp.sin(ang_r), -jnp.sin(ang_c), jnp.sin(ang_c)], axis=-1)
    return jnp.concatenate([cos, sin], axis=-1)


def _rotate_rope(r, tab, lane_lo):
    rot = r * tab
    rot = rot + pltpu.roll(rot, ROPE_D, 1)
    return jnp.where(lane_lo, rot, 0.0)


def _qup_kernel(cq_ref, nw_ref, w_ref, gn_ref, gr_ref, tab_ref, o_ref, h_ref):
    @pl.when(pl.program_id(1) == 0)
    def _():
        h_ref[...] = _rms(cq_ref[...].astype(F32), nw_ref[...]).astype(BF16)

    y = _dot(h_ref[...], w_ref[...])
    hw = NOPE_D + 2 * ROPE_D
    scale = (NOPE_D + ROPE_D) ** -0.5 * LOG2E
    lane_lo = lax.broadcasted_iota(jnp.int32, (1, LANES), 1) < ROPE_D
    tab = tab_ref[...] * gr_ref[...]
    for hh in range(y.shape[1] // hw):
        yn = y[:, hh * hw:hh * hw + NOPE_D]
        yr = y[:, hh * hw + NOPE_D:(hh + 1) * hw]
        o_ref[:, hh * hw:hh * hw + NOPE_D] = (_rms(yn, gn_ref[...]) * scale).astype(o_ref.dtype)
        ms = jnp.sum(jnp.where(lane_lo, yr * yr, 0.0), axis=-1, keepdims=True) * (1.0 / ROPE_D)
        qr = _rotate_rope(yr, tab, lane_lo) * (lax.rsqrt(ms + EPS) * scale)
        o_ref[:, hh * hw + NOPE_D:(hh + 1) * hw] = qr.astype(o_ref.dtype)


def _q_up(proj, col_blk, q_lora, seq, a_norm, w, gain_n, gain_r, table):
    m = proj.shape[0]
    n = w.shape[1]
    tm = _pick_tile(seq, ROW_TILE)
    tn = _pick_tile(n, COL_TILE)
    nt = table.shape[0] // tm
    return pl.pallas_call(
        _qup_kernel,
        grid=(m // tm, n // tn),
        in_specs=[
            pl.BlockSpec((tm, q_lora), lambda i, j: (i, col_blk)),
            pl.BlockSpec((1, q_lora), lambda i, j: (0, 0)),
            pl.BlockSpec((q_lora, tn), lambda i, j: (0, j)),
            pl.BlockSpec((1, NOPE_D), lambda i, j: (0, 0)),
            pl.BlockSpec((1, 2 * ROPE_D), lambda i, j: (0, 0)),
            pl.BlockSpec((tm, 2 * ROPE_D), lambda i, j: (i % nt, 0)),
        ],
        out_specs=pl.BlockSpec((tm, tn), lambda i, j: (i, j)),
        out_shape=jax.ShapeDtypeStruct((m, n), BF16),
        scratch_shapes=[pltpu.VMEM((tm, q_lora), BF16)],
        compiler_params=_cparams(2),
        name="mla_q_up",
    )(proj, a_norm.reshape(1, q_lora), w, gain_n.reshape(1, NOPE_D), gain_r.reshape(1, 2 * ROPE_D), table)


def _kvup_kernel(*refs, normalize_in, emit_cache, heads):
    refs = list(refs)
    ckv_ref, kr_ref, nw_ref, gkr_ref, w_ref, gk_ref, tab_ref, k_ref, v_ref = refs[:9]
    lane_lo = lax.broadcasted_iota(jnp.int32, (1, LANES), 1) < ROPE_D
    c = ckv_ref[...].astype(F32)
    r = kr_ref[...].astype(F32)
    if normalize_in:
        c = _rms(c, nw_ref[...])
        ms = jnp.sum(jnp.where(lane_lo, r * r, 0.0), axis=-1, keepdims=True) * (1.0 / ROPE_D)
        r = r * lax.rsqrt(ms + EPS) * gkr_ref[...]
    if emit_cache:
        refs[9][...] = c
        refs[10][...] = r[:, :ROPE_D]
    k_rope = _rotate_rope(r, tab_ref[...], lane_lo).astype(k_ref.dtype)
    y = _dot(c.astype(BF16), w_ref[...])
    hw = 2 * LANES
    for h in range(heads):
        k_ref[:, h * hw:h * hw + NOPE_D] = _rms(y[:, h * hw:h * hw + NOPE_D], gk_ref[...]).astype(k_ref.dtype)
        k_ref[:, h * hw + NOPE_D:(h + 1) * hw] = k_rope
        v_ref[:, h * LANES:(h + 1) * LANES] = y[:, h * hw + NOPE_D:(h + 1) * hw].astype(v_ref.dtype)


def _kv_up(ckv_src, ckv_blk, kr_src, kr_blk, seq, kv_lora, heads, a_norm, gain_kr, w, gain_k, table,
           *, normalize_in, emit_cache):
    m = ckv_src.shape[0]
    tm = _pick_tile(seq, ROW_TILE)
    nt = table.shape[0] // tm
    out_specs = [
        pl.BlockSpec((tm, heads * 2 * LANES), lambda i: (i, 0)),
        pl.BlockSpec((tm, heads * LANES), lambda i: (i, 0)),
    ]
    out_shape = [
        jax.ShapeDtypeStruct((m, heads * 2 * LANES), BF16),
        jax.ShapeDtypeStruct((m, heads * LANES), BF16),
    ]
    if emit_cache:
        out_specs += [pl.BlockSpec((tm, kv_lora), lambda i: (i, 0)), pl.BlockSpec((tm, ROPE_D), lambda i: (i, 0))]
        out_shape += [jax.ShapeDtypeStruct((m, kv_lora), F32), jax.ShapeDtypeStruct((m, ROPE_D), F32)]
    return pl.pallas_call(
        functools.partial(_kvup_kernel, normalize_in=normalize_in, emit_cache=emit_cache, heads=heads),
        grid=(m // tm,),
        in_specs=[
            pl.BlockSpec((tm, kv_lora), lambda i: (i, ckv_blk)),
            pl.BlockSpec((tm, 2 * ROPE_D), lambda i: (i, kr_blk)),
            pl.BlockSpec((1, kv_lora), lambda i: (0, 0)),
            pl.BlockSpec((1, 2 * ROPE_D), lambda i: (0, 0)),
            pl.BlockSpec((kv_lora, heads * 2 * LANES), lambda i: (0, 0)),
            pl.BlockSpec((1, NOPE_D), lambda i: (0, 0)),
            pl.BlockSpec((tm, 2 * ROPE_D), lambda i: (i % nt, 0)),
        ],
        out_specs=out_specs,
        out_shape=out_shape,
        compiler_params=_cparams(1),
        name="mla_kv_up",
    )(ckv_src, kr_src, a_norm.reshape(1, kv_lora), gain_kr.reshape(1, 2 * ROPE_D), w,
      gain_k.reshape(1, NOPE_D), table)


def _mla_attn_kernel(*refs, has_ctx):
    if has_ctx:
        q_ref, k_ref, v_ref, kc_ref, vc_ref, g_ref, o_ref = refs
    else:
        q_ref, k_ref, v_ref, g_ref, o_ref = refs
    qw = 2 * LANES
    hp = q_ref.shape[-1] // qw
    scores = []
    for hh in range(hp):
        q = q_ref[:, hh * qw:(hh + 1) * qw]
        s = _dot_nt(q, k_ref[:, hh * qw:(hh + 1) * qw])
        s_c = _dot_nt(q, kc_ref[:, hh * qw:(hh + 1) * qw]) if has_ctx else None
        scores.append((s, s_c))
    for hh, (s, s_c) in enumerate(scores):
        sl = slice(hh * LANES, (hh + 1) * LANES)
        mx = jnp.max(s, axis=-1, keepdims=True)
        if has_ctx:
            mx = jnp.maximum(mx, jnp.max(s_c, axis=-1, keepdims=True))
        p = jnp.exp2(s - mx)
        l = jnp.sum(p, axis=-1, keepdims=True)
        acc = _dot(p.astype(BF16), v_ref[:, sl])
        if has_ctx:
            p_c = jnp.exp2(s_c - mx)
            l = l + jnp.sum(p_c, axis=-1, keepdims=True)
            acc = acc + _dot(p_c.astype(BF16), vc_ref[:, sl])
        o_ref[:, sl] = (acc / l * _silu(g_ref[:, sl].astype(F32))).astype(o_ref.dtype)


def _mla_attention(q, k, v, proj, gate_blk, batch, seq, heads, hp, ctx=None):
    tq = _pick_tile(seq, ATTN_Q)
    nq = seq // tq
    assert heads % hp == 0 and gate_blk % hp == 0
    qmap = lambda b, h, t: (b * nq + t, h)
    in_specs = [
        pl.BlockSpec((tq, hp * 2 * LANES), qmap),
        pl.BlockSpec((seq, hp * 2 * LANES), lambda b, h, t: (b, h)),
        pl.BlockSpec((seq, hp * LANES), lambda b, h, t: (b, h)),
    ]
    args = [q, k, v]
    if ctx is not None:
        k_c, v_c = ctx
        lc = k_c.shape[0] // batch
        in_specs += [
            pl.BlockSpec((lc, hp * 2 * LANES), lambda b, h, t: (b, h)),
            pl.BlockSpec((lc, hp * LANES), lambda b, h, t: (b, h)),
        ]
        args += [k_c, v_c]
    in_specs.append(pl.BlockSpec((tq, hp * LANES), lambda b, h, t: (b * nq + t, gate_blk // hp + h)))
    args.append(proj)
    return pl.pallas_call(
        functools.partial(_mla_attn_kernel, has_ctx=ctx is not None),
        grid=(batch, heads // hp, nq),
        in_specs=in_specs,
        out_specs=pl.BlockSpec((tq, hp * LANES), qmap),
        out_shape=jax.ShapeDtypeStruct((batch * seq, heads * LANES), BF16),
        compiler_params=_cparams(3),
        name="mla_attention",
    )(*args)


def _pad_cols(w, mult):
    pad = (-w.shape[1]) % mult
    return w if pad == 0 else jnp.concatenate([w, jnp.zeros((w.shape[0], pad), w.dtype)], axis=1)


def kernel(x_prompt, x_sample, cache_na_k, cache_na_v, state_hgrn, cache_mla_ckv, cache_mla_krope, c, c_ctx,
           norm_w, w_ada, b_ada, w_out,
           w_in_ab, na_q_norm, na_k_norm, na_rpb, sg_norm, sg_w, sg_b,
           w_in_cd, hgrn_lb, hgrn_out_norm, mla_q_a_norm, mla_w_q_up, mla_kv_a_norm, mla_w_kv_up,
           mla_q_norm, mla_k_norm):
    bp, lp, d = x_prompt.shape
    bs, ls, _ = x_sample.shape
    depth = norm_w.shape[0]
    hd = cache_na_k.shape[-1]
    heads = cache_na_k.shape[-2]
    width = heads * hd
    assert hd == LANES and state_hgrn.shape[-2:] == (hd, hd) and ls % GRID_W == 0
    q_lora = mla_q_a_norm.shape[-1]
    kv_lora = mla_kv_a_norm.shape[-1]
    assert mla_q_norm.shape[-1] == NOPE_D + ROPE_D and cache_mla_krope.shape[-1] == ROPE_D
    assert mla_w_kv_up.shape[-1] == heads * 2 * LANES

    n_cond = bs + 1
    cond_rows = -(-n_cond // 8) * 8
    cond = jnp.concatenate([c, c_ctx[None, :], jnp.zeros((cond_rows - n_cond, d), F32)], axis=0)
    mod = _modulation(cond, w_ada, b_ada).reshape(depth, cond_rows, 3, 1, d)

    lb_soft = jax.nn.softmax(hgrn_lb.astype(F32), axis=0)
    lb_cum = jnp.cumsum(lb_soft, axis=0)
    lower_bounds = lb_cum - lb_cum[:1]

    xp = x_prompt.reshape(bp * lp, d)
    xs = x_sample.reshape(bs * ls, d)
    na_k_new, na_v_new, hgrn_new, ckv_new, kr_new = [], [], [], [], []
    for layer in range(depth):
        j = layer // 2
        shift_s, scale_s, gate_s = (mod[layer, :bs, i] for i in range(3))
        shift_p, scale_p, gate_p = (mod[layer, bs:bs + 1, i] for i in range(3))
        w_o = w_out[layer].astype(BF16)
        if layer % 2 == 0:
            w_in = _pad_cols(w_in_ab[j].astype(BF16), COL_TILE)
            proj_p = _in_proj(xp, lp, norm_w[layer], scale_p, shift_p, w_in)
            proj_s = _in_proj(xs, ls, norm_w[layer], scale_s, shift_s, w_in)
            mix_a_p, k_new, v_new = _attn_ctx0(proj_p, bp, lp, heads, na_q_norm[j], na_k_norm[j])
            na_k_new.append(k_new.reshape(bp, lp, heads, hd))
            na_v_new.append(v_new.reshape(bp, lp, heads, hd))
            bias = _na_bias_tables(na_rpb[j], ls // GRID_W)
            lc = cache_na_k.shape[2]
            mix_a_s = _na_latent(proj_s, bs, ls, heads,
                                 cache_na_k[:, j].reshape(bs, lc, width).astype(BF16),
                                 cache_na_v[:, j].reshape(bs, lc, width).astype(BF16),
                                 bias, na_q_norm[j], na_k_norm[j])
            mix_b_p = _spatial_gating(proj_p, width, sg_norm[j], sg_w[j], sg_b[j])
            mix_b_s = _spatial_gating(proj_s, width, sg_norm[j], sg_w[j], sg_b[j])
        else:
            w_cd = w_in_cd[j].astype(BF16)
            kr_src = 5 * width + q_lora + kv_lora
            w_kr = w_cd[:, kr_src:kr_src + ROPE_D]
            w_in = jnp.concatenate([w_cd[:, :5 * width], w_cd[:, kr_src + ROPE_D:], w_cd[:, 5 * width:kr_src],
                                    w_kr, w_kr[:, _ROPE_SWAP]], axis=1)
            w_in = _pad_cols(w_in, COL_TILE)
            assert (6 * width) % q_lora == 0 and (6 * width + q_lora) % kv_lora == 0
            cq_blk = (6 * width) // q_lora
            ckv_blk = (6 * width + q_lora) // kv_lora
            kr_blk = (6 * width + q_lora + kv_lora) // LANES
            gd_blk = (5 * width) // LANES
            proj_p = _in_proj(xp, lp, norm_w[layer], scale_p, shift_p, w_in)
            proj_s = _in_proj(xs, ls, norm_w[layer], scale_s, shift_s, w_in)

            lb_f, lb_b = lower_bounds[layer, 0], lower_bounds[layer, 1]
            of_p, st_f = _hgrn_pass(proj_p, bp, lp, heads, lb_f, None, rev=False, emit_state=True)
            mix_a_p, st_b = _hgrn_pass(proj_p, bp, lp, heads, lb_b, None, rev=True, emit_state=True,
                                       prev=of_p, out_norm=hgrn_out_norm[j])
            hgrn_new.append(jnp.stack([st_f, st_b], axis=1))
            of_s, _ = _hgrn_pass(proj_s, bs, ls, heads, lb_f, state_hgrn[:, j, 0], rev=False, emit_state=False)
            mix_a_s, _ = _hgrn_pass(proj_s, bs, ls, heads, lb_b, state_hgrn[:, j, 1], rev=True, emit_state=False,
                                    prev=of_s, out_norm=hgrn_out_norm[j])

            wq = mla_w_q_up[j].reshape(q_lora, heads, NOPE_D + ROPE_D)
            wq = jnp.concatenate([wq, wq[:, :, NOPE_D:][:, :, _ROPE_SWAP]], axis=-1)
            wq = wq.reshape(q_lora, heads * 2 * LANES).astype(BF16)
            wkv = mla_w_kv_up[j].astype(BF16)
            gq, gk = mla_q_norm[j], mla_k_norm[j]
            gq_r = jnp.concatenate([gq[NOPE_D:], gq[NOPE_D:][_ROPE_SWAP]])
            gk_r = jnp.concatenate([gk[NOPE_D:], gk[NOPE_D:][_ROPE_SWAP]])
            tab_p = _rope_table(_pick_tile(lp, ROW_TILE), False)
            tab_s = _rope_table(ls, True)
            tab_c = _rope_table(_pick_tile(cache_mla_ckv.shape[2], ROW_TILE), False)

            q_p = _q_up(proj_p, cq_blk, q_lora, lp, mla_q_a_norm[j], wq, gq[:NOPE_D], gq_r, tab_p)
            k_p, v_p, ckv_n, kr_n = _kv_up(proj_p, ckv_blk, proj_p, kr_blk, lp, kv_lora, heads, mla_kv_a_norm[j],
                                           gk_r, wkv, gk[:NOPE_D], tab_p, normalize_in=True, emit_cache=True)
            ckv_new.append(ckv_n.reshape(bp, lp, kv_lora))
            kr_new.append(kr_n.reshape(bp, lp, ROPE_D))
            mix_b_p = _mla_attention(q_p, k_p, v_p, proj_p, gd_blk, bp, lp, heads, CTX_ATTN_HEADS)

            q_s = _q_up(proj_s, cq_blk, q_lora, ls, mla_q_a_norm[j], wq, gq[:NOPE_D], gq_r, tab_s)
            k_s, v_s = _kv_up(proj_s, ckv_blk, proj_s, kr_blk, ls, kv_lora, heads, mla_kv_a_norm[j],
                              gk_r, wkv, gk[:NOPE_D], tab_s, normalize_in=True, emit_cache=False)
            lc = cache_mla_ckv.shape[2]
            kr_c = cache_mla_krope[:, j].reshape(bs * lc, ROPE_D)
            kr_c = jnp.concatenate([kr_c, jnp.zeros_like(kr_c)], axis=1)
            k_c, v_c = _kv_up(cache_mla_ckv[:, j].reshape(bs * lc, kv_lora), 0, kr_c, 0, lc, kv_lora, heads,
                              mla_kv_a_norm[j], gk_r, wkv, gk[:NOPE_D], tab_c, normalize_in=False, emit_cache=False)
            mix_b_s = _mla_attention(q_s, k_s, v_s, proj_s, gd_blk, bs, ls, heads, LAT_ATTN_HEADS, ctx=(k_c, v_c))
        xp = _out_proj(mix_a_p, mix_b_p, w_o, xp, lp, gate_p)
        xs = _out_proj(mix_a_s, mix_b_s, w_o, xs, ls, gate_s)
    return (xp.reshape(bp, lp, d), xs.reshape(bs, ls, d),
            jnp.stack(na_k_new, axis=1), jnp.stack(na_v_new, axis=1), jnp.stack(hgrn_new, axis=1),
            jnp.stack(ckv_new, axis=1), jnp.stack(kr_new, axis=1))
```

```python
import functools

import numpy as np
import jax
import jax.numpy as jnp
from jax import lax
from jax.experimental import pallas as pl
from jax.experimental.pallas import tpu as pltpu

F32 = jnp.float32
BF16 = jnp.bfloat16

GRID_W = 64
WIN_R = 8
WIN_C = 16
SG_CHUNK = 128
NOPE_D = 128
ROPE_D = 64
ROPE_BASE = 10000.0
EPS = 1e-6
NEG_INF = -1e30

LANES = 128
VMEM_LIMIT = 56 * 1024 * 1024

ROW_TILE = 512
MM_ROW_TILE = 1024
COL_TILE = 1024
NA_ROWS = 4
NA_SPAN = NA_ROWS + WIN_R
ATTN_Q = 256
SCAN_BLOCK = 256
SCAN_CHUNK = 32
SCAN_HEADS = 8
CTX_ATTN_HEADS = 4
LAT_ATTN_HEADS = 2
LOG2E = 1.4426950408889634
EXP_CLAMP = 80.0


def _cparams(n_axes):
    return pltpu.CompilerParams(
        dimension_semantics=("arbitrary",) * n_axes, vmem_limit_bytes=VMEM_LIMIT)


def _pick_tile(n, pref):
    t = pref
    while n % t:
        t //= 2
    assert t >= LANES or t == n, (n, pref)
    return t


def _rms(x, w):
    return x * lax.rsqrt(jnp.mean(x * x, axis=-1, keepdims=True) + EPS) * w


def _silu(x):
    return x * jax.nn.sigmoid(x)


def _gelu(x):
    return 0.5 * x * (1.0 + jnp.tanh(0.7978845608028654 * (x + 0.044715 * (x * x * x))))


def _dot(a, b):
    return jnp.dot(a, b, preferred_element_type=F32)


def _dot_nt(a, b):
    return lax.dot_general(a, b, (((1,), (1,)), ((), ())), preferred_element_type=F32)


def _dot_tn(a, b):
    return lax.dot_general(a, b, (((0,), (0,)), ((), ())), preferred_element_type=F32)


def _mod_kernel(c_ref, w_ref, b_ref, o_ref):
    a = _silu(c_ref[...]).astype(BF16)
    o_ref[0] = _dot(a, w_ref[0].astype(BF16)) + b_ref[0]


def _modulation(cond, w_ada, b_ada):
    depth, d, n = w_ada.shape
    r = cond.shape[0]
    tn = _pick_tile(n, 512)
    return pl.pallas_call(
        _mod_kernel,
        grid=(depth, n // tn),
        in_specs=[
            pl.BlockSpec((r, d), lambda l, j: (0, 0)),
            pl.BlockSpec((1, d, tn), lambda l, j: (l, 0, j)),
            pl.BlockSpec((1, 1, tn), lambda l, j: (l, 0, j)),
        ],
        out_specs=pl.BlockSpec((1, r, tn), lambda l, j: (l, 0, j)),
        out_shape=jax.ShapeDtypeStruct((depth, r, n), F32),
        compiler_params=_cparams(2),
        name="adaln_modulation",
    )(cond, w_ada, b_ada.reshape(depth, 1, n))


def _norm_mod_kernel(x_ref, nw_ref, sc_ref, sh_ref, o_ref):
    h = _rms(x_ref[...], nw_ref[...])
    o_ref[...] = (h * (1.0 + sc_ref[0]) + sh_ref[0]).astype(o_ref.dtype)


def _matmul_kernel(a_ref, w_ref, o_ref):
    o_ref[...] = _dot(a_ref[...], w_ref[...]).astype(o_ref.dtype)


def _in_proj(x2, seq, norm_w, scale, shift, w):
    m, d = x2.shape
    n = w.shape[1]
    nb = scale.shape[0]
    tr = _pick_tile(m, ROW_TILE)
    if nb == 1:
        cond_map = lambda i: (0, 0, 0)
    else:
        assert seq % tr == 0
        cond_map = lambda i: ((i * tr) // seq, 0, 0)
    h = pl.pallas_call(
        _norm_mod_kernel,
        grid=(m // tr,),
        in_specs=[
            pl.BlockSpec((tr, d), lambda i: (i, 0)),
            pl.BlockSpec((1, d), lambda i: (0, 0)),
            pl.BlockSpec((1, 1, d), cond_map),
            pl.BlockSpec((1, 1, d), cond_map),
        ],
        out_specs=pl.BlockSpec((tr, d), lambda i: (i, 0)),
        out_shape=jax.ShapeDtypeStruct((m, d), BF16),
        compiler_params=_cparams(1),
        name="norm_modulate",
    )(x2, norm_w.reshape(1, d), scale, shift)
    tm = _pick_tile(m, MM_ROW_TILE)
    tn = _pick_tile(n, COL_TILE)
    return pl.pallas_call(
        _matmul_kernel,
        grid=(m // tm, n // tn),
        in_specs=[
            pl.BlockSpec((tm, d), lambda i, j: (i, 0)),
            pl.BlockSpec((d, tn), lambda i, j: (0, j)),
        ],
        out_specs=pl.BlockSpec((tm, tn), lambda i, j: (i, j)),
        out_shape=jax.ShapeDtypeStruct((m, n), BF16),
        compiler_params=_cparams(2),
        name="in_proj",
    )(h, w)


def _out_kernel(a_ref, b_ref, wa_ref, wb_ref, x_ref, g_ref, o_ref):
    acc = _dot(a_ref[...], wa_ref[...]) + _dot(b_ref[...], wb_ref[...])
    o_ref[...] = x_ref[...] + g_ref[0] * acc


def _out_proj(mix_a, mix_b, w_out, x2, seq, gate):
    m, d = x2.shape
    wdt = mix_a.shape[1]
    nb = gate.shape[0]
    tm = _pick_tile(m, MM_ROW_TILE)
    tn = _pick_tile(d, COL_TILE)
    if nb == 1:
        cond_map = lambda i, j: (0, 0, j)
    else:
        assert seq % tm == 0
        cond_map = lambda i, j: ((i * tm) // seq, 0, j)
    return pl.pallas_call(
        _out_kernel,
        grid=(m // tm, d // tn),
        in_specs=[
            pl.BlockSpec((tm, wdt), lambda i, j: (i, 0)),
            pl.BlockSpec((tm, wdt), lambda i, j: (i, 0)),
            pl.BlockSpec((wdt, tn), lambda i, j: (0, j)),
            pl.BlockSpec((wdt, tn), lambda i, j: (1, j)),
            pl.BlockSpec((tm, tn), lambda i, j: (i, j)),
            pl.BlockSpec((1, 1, tn), cond_map),
        ],
        out_specs=pl.BlockSpec((tm, tn), lambda i, j: (i, j)),
        out_shape=jax.ShapeDtypeStruct((m, d), F32),
        compiler_params=_cparams(2),
        name="out_proj_residual",
    )(mix_a, mix_b, w_out, w_out, x2, gate)


def _attn_ctx0_kernel(q_ref, k_ref, v_ref, g_ref, qw_ref, kw_ref, o_ref, ko_ref, vo_ref):
    hd = LANES
    qscale = hd ** -0.5 * LOG2E
    scores = []
    for hh in range(q_ref.shape[-1] // hd):
        sl = slice(hh * hd, (hh + 1) * hd)
        qn = _rms(q_ref[:, sl].astype(F32), qw_ref[...]) * qscale
        kn = _rms(k_ref[:, sl].astype(F32), kw_ref[...])
        ko_ref[:, sl] = kn
        vo_ref[:, sl] = v_ref[:, sl].astype(F32)
        scores.append(_dot_nt(qn.astype(BF16), kn.astype(BF16)))
    for hh, s in enumerate(scores):
        sl = slice(hh * hd, (hh + 1) * hd)
        p = jnp.exp2(s - jnp.max(s, axis=-1, keepdims=True))
        l = jnp.sum(p, axis=-1, keepdims=True)
        o = _dot(p.astype(BF16), v_ref[:, sl]) / l
        o_ref[:, sl] = (o * _silu(g_ref[:, sl].astype(F32))).astype(o_ref.dtype)


def _attn_ctx0(proj, batch, seq, heads, q_norm, k_norm):
    hd = LANES
    m = batch * seq
    hp = CTX_ATTN_HEADS
    assert heads % hp == 0
    ng = heads // hp
    blk = lambda off: pl.BlockSpec((seq, hp * hd), lambda b, h: (b, off * ng + h))
    vec = pl.BlockSpec((1, hd), lambda b, h: (0, 0))
    return pl.pallas_call(
        _attn_ctx0_kernel,
        grid=(batch, ng),
        in_specs=[blk(0), blk(1), blk(2), blk(3), vec, vec],
        out_specs=[blk(0), blk(0), blk(0)],
        out_shape=[
            jax.ShapeDtypeStruct((m, heads * hd), BF16),
            jax.ShapeDtypeStruct((m, heads * hd), F32),
            jax.ShapeDtypeStruct((m, heads * hd), F32),
        ],
        compiler_params=_cparams(2),
        name="ctx_attention_l0",
    )(proj, proj, proj, proj, q_norm.reshape(1, hd), k_norm.reshape(1, hd))


def _na_bias_plan(rows):
    tiles = rows // NA_ROWS
    assert min(WIN_R, rows) == WIN_R and tiles >= 3
    t_var = np.array([0, 1, tiles - 1])
    ks = np.clip(NA_ROWS * t_var - WIN_R // 2, 0, rows - NA_SPAN)
    r = NA_ROWS * t_var[:, None] + np.arange(NA_ROWS)[None, :]
    rs = np.clip(r - WIN_R // 2, 0, rows - WIN_R)
    kr = ks[:, None] + np.arange(NA_SPAN)[None, :]
    row_ok = (kr[:, None, :] >= rs[:, :, None]) & (kr[:, None, :] < rs[:, :, None] + WIN_R)
    dr = kr[:, None, :] - r[:, :, None] + WIN_R - 1
    return np.where(row_ok, dr, -1)


def _na_bias_kernel(rpb_ref, o_ref, *, plan):
    h = pl.program_id(0)
    n_dr, n_dc = 2 * WIN_R - 1, 2 * WIN_C - 1
    qc = lax.broadcasted_iota(jnp.int32, (GRID_W, GRID_W), 0)
    kc = lax.broadcasted_iota(jnp.int32, (GRID_W, GRID_W), 1)
    cstart = jnp.clip(qc - WIN_C // 2, 0, GRID_W - WIN_C)
    col_ok = (kc >= cstart) & (kc < cstart + WIN_C)
    dc = jnp.clip(kc - qc + WIN_C - 1, 0, n_dc - 1)
    neg = jnp.full((GRID_W, GRID_W), NEG_INF, F32)
    slabs = {-1: neg}
    for dr in sorted(set(int(v) for v in plan.ravel()) - {-1}):
        acc = neg
        for j in range(n_dc):
            acc = jnp.where(col_ok & (dc == j), rpb_ref[(h * n_dr + dr) * n_dc + j] * LOG2E, acc)
        slabs[dr] = acc
    for var in range(plan.shape[0]):
        for i in range(plan.shape[1]):
            row = jnp.concatenate([slabs[int(v)] for v in plan[var, i]], axis=1)
            o_ref[0, var, i * GRID_W:(i + 1) * GRID_W, :] = row


def _na_bias_tables(rpb, rows):
    heads = rpb.shape[0]
    shape = (heads, 3, NA_ROWS * GRID_W, NA_SPAN * GRID_W)
    return pl.pallas_call(
        functools.partial(_na_bias_kernel, plan=_na_bias_plan(rows)),
        grid=(heads,),
        in_specs=[pl.BlockSpec(memory_space=pltpu.SMEM)],
        out_specs=pl.BlockSpec((1,) + shape[1:], lambda h: (h, 0, 0, 0)),
        out_shape=jax.ShapeDtypeStruct(shape, F32),
        compiler_params=_cparams(1),
        name="na_bias_tables",
    )(rpb.astype(F32).reshape(-1))


def _na_kernel(q_ref, k_ref, v_ref, g_ref, kc_ref, vc_ref, b_ref, qw_ref, kw_ref, o_ref, kn_ref, *, rows):
    t = pl.program_id(2)
    hd = LANES
    hp = q_ref.shape[-1] // hd

    @pl.when(t == 0)
    def _():
        for hh in range(hp):
            sl = slice(hh * hd, (hh + 1) * hd)
            kn_ref[:, sl] = _rms(k_ref[:, sl].astype(F32), kw_ref[...]).astype(BF16)

    ks = jnp.clip(NA_ROWS * t - WIN_R // 2, 0, rows - NA_SPAN)
    span = pl.ds(pl.multiple_of(ks * GRID_W, GRID_W), NA_SPAN * GRID_W)
    qscale = hd ** -0.5 * LOG2E
    scores = []
    for hh in range(hp):
        sl = slice(hh * hd, (hh + 1) * hd)
        qn = (_rms(q_ref[:, sl].astype(F32), qw_ref[...]) * qscale).astype(BF16)
        scores.append((_dot_nt(qn, kn_ref[span, sl]) + b_ref[hh, 0], _dot_nt(qn, kc_ref[0, :, sl])))
    for hh, (s_lat, s_ctx) in enumerate(scores):
        sl = slice(hh * hd, (hh + 1) * hd)
        mx = jnp.maximum(jnp.max(s_lat, axis=-1, keepdims=True), jnp.max(s_ctx, axis=-1, keepdims=True))
        p_lat = jnp.exp2(s_lat - mx)
        p_ctx = jnp.exp2(s_ctx - mx)
        l = jnp.sum(p_lat, axis=-1, keepdims=True) + jnp.sum(p_ctx, axis=-1, keepdims=True)
        o = (_dot(p_lat.astype(BF16), v_ref[span, sl]) + _dot(p_ctx.astype(BF16), vc_ref[0, :, sl])) / l
        o_ref[:, sl] = (o * _silu(g_ref[:, sl].astype(F32))).astype(o_ref.dtype)


def _na_latent(proj, batch, seq, heads, k_ctx, v_ctx, bias, q_norm, k_norm):
    hd = LANES
    rows = seq // GRID_W
    tiles = rows // NA_ROWS
    tq = NA_ROWS * GRID_W
    lc = k_ctx.shape[1]
    hp = LAT_ATTN_HEADS
    assert heads % hp == 0
    ng = heads // hp
    qblk = lambda off: pl.BlockSpec((tq, hp * hd), lambda b, h, t: (b * tiles + t, off * ng + h))
    full = lambda off: pl.BlockSpec((seq, hp * hd), lambda b, h, t: (b, off * ng + h))
    ctx = pl.BlockSpec((1, lc, hp * hd), lambda b, h, t: (b, 0, h))
    variant = lambda t: jnp.where(t == 0, 0, jnp.where(t == tiles - 1, 2, 1))
    vec = pl.BlockSpec((1, hd), lambda b, h, t: (0, 0))
    return pl.pallas_call(
        functools.partial(_na_kernel, rows=rows),
        grid=(batch, ng, tiles),
        in_specs=[
            qblk(0), full(1), full(2), qblk(3), ctx, ctx,
            pl.BlockSpec((hp, 1, tq, NA_SPAN * GRID_W), lambda b, h, t: (h, variant(t), 0, 0)),
            vec, vec,
        ],
        out_specs=qblk(0),
        out_shape=jax.ShapeDtypeStruct((batch * seq, heads * hd), BF16),
        scratch_shapes=[pltpu.VMEM((seq, hp * hd), BF16)],
        compiler_params=_cparams(3),
        name="neighbourhood_attention",
    )(proj, proj, proj, proj, k_ctx, v_ctx, bias, q_norm.reshape(1, hd), k_norm.reshape(1, hd))


def _sg_kernel(u_ref, v_ref, g_ref, nw_ref, w_ref, b_ref, o_ref):
    groups = w_ref.shape[0]
    hd = w_ref.shape[-1]
    u = _gelu(u_ref[...].astype(F32))
    v = _rms(_gelu(v_ref[...].astype(F32)), nw_ref[...]).astype(BF16)
    g = _silu(g_ref[...].astype(F32))
    for gi in range(groups):
        sl = slice(gi * hd, (gi + 1) * hd)
        mixed = _dot(w_ref[gi], v[:, sl]) + b_ref[gi]
        o_ref[:, sl] = (u[:, sl] * mixed * g[:, sl]).astype(o_ref.dtype)


def _spatial_gating(proj, width, sg_norm, sg_w, sg_b):
    m = proj.shape[0]
    groups = sg_w.shape[0]
    blk = lambda off: pl.BlockSpec((SG_CHUNK, width), lambda i: (i, off))
    bias = jnp.broadcast_to(sg_b.astype(F32)[:, :, None], (groups, SG_CHUNK, LANES))
    return pl.pallas_call(
        _sg_kernel,
        grid=(m // SG_CHUNK,),
        in_specs=[
            blk(4), blk(5), blk(6),
            pl.BlockSpec((1, width), lambda i: (0, 0)),
            pl.BlockSpec((groups, SG_CHUNK, SG_CHUNK), lambda i: (0, 0, 0)),
            pl.BlockSpec((groups, SG_CHUNK, LANES), lambda i: (0, 0, 0)),
        ],
        out_specs=pl.BlockSpec((SG_CHUNK, width), lambda i: (i, 0)),
        out_shape=jax.ShapeDtypeStruct((m, width), BF16),
        compiler_params=_cparams(1),
        name="spatial_gating",
    )(proj, proj, proj, sg_norm.reshape(1, width), sg_w.astype(BF16), bias)


def _scan_matrix(rev):
    t = np.arange(SCAN_BLOCK)
    same = (t[:, None] // SCAN_CHUNK) == (t[None, :] // SCAN_CHUNK)
    incl = same & ((t[None, :] >= t[:, None]) if rev else (t[None, :] <= t[:, None]))
    return jnp.asarray(incl, dtype=BF16)


def _chunk_rows(a, pos):
    parts = []
    for c in range(a.shape[0] // SCAN_CHUNK):
        row = a[c * SCAN_CHUNK + pos:c * SCAN_CHUNK + pos + 1]
        parts.append(jnp.broadcast_to(row, (SCAN_CHUNK, a.shape[1])))
    return jnp.concatenate(parts, axis=0)


def _hgrn_block(q, z, v, lb, tri, states, rev):
    hd = LANES
    nc = q.shape[0] // SCAN_CHUNK
    half = SCAN_CHUNK // 2
    f = lb + (1.0 - lb) * jax.nn.sigmoid(z)
    g = jnp.log(f)
    k = 1.0 - f
    g_hi = g.astype(BF16)
    g_lo = (g - g_hi.astype(F32)).astype(BF16)
    a = _dot(tri, g_hi) + _dot(tri, g_lo)
    tot_row = 0 if rev else SCAN_CHUNK - 1
    a_ref = _chunk_rows(a, half if rev else half - 1)
    a_tot = _chunk_rows(a, tot_row)
    qe = (q * jnp.exp(jnp.minimum(a - a_ref, EXP_CLAMP))).astype(BF16)
    ke = (k * jnp.exp(jnp.minimum(a_ref - a, EXP_CLAMP))).astype(BF16)
    qa = (q * jnp.exp(a)).astype(BF16)
    kd = (k * jnp.exp(a_tot - a)).astype(BF16)
    heads = [slice(h * hd, (h + 1) * hd) for h in range(len(states))]
    chunks = [slice(c * SCAN_CHUNK, (c + 1) * SCAN_CHUNK) for c in range(nc)]
    dec = [jnp.exp(a[c * SCAN_CHUNK + tot_row:c * SCAN_CHUNK + tot_row + 1]) for c in range(nc)]
    probs = [_dot_nt(qe[:, sl], ke[:, sl]) for sl in heads]
    upd = [[_dot_tn(v[cs, sl], kd[cs, sl]) for cs in chunks] for sl in heads]
    in_chunk = tri > 0
    intra = [_dot(jnp.where(in_chunk, p.astype(BF16), 0.0), v[:, sl]) for p, sl in zip(probs, heads)]
    outs, new_states = [], []
    for h, sl in enumerate(heads):
        state = states[h]
        starts = [None] * nc
        for c in (range(nc - 1, -1, -1) if rev else range(nc)):
            starts[c] = state.astype(BF16)
            state = state * dec[c][:, sl] + upd[h][c]
        inter = [_dot_nt(qa[cs, sl], starts[c]) for c, cs in enumerate(chunks)]
        outs.append(intra[h] + jnp.concatenate(inter, axis=0))
        new_states.append(state)
    return outs, new_states


def _hgrn_kernel(*refs, rev, has_s0, emit_state, final):
    refs = list(refs)
    q_ref, z_ref, i_ref, lb_ref, tri_ref = refs[:5]
    del refs[:5]
    s0_ref = refs.pop(0) if has_s0 else None
    if final:
        of_ref, gate_ref, onw_ref = refs[:3]
        del refs[:3]
    o_ref = refs.pop(0)
    st_ref = refs.pop(0) if emit_state else None
    state_ref = refs.pop(0)
    t = pl.program_id(2)
    hd = LANES
    tri = tri_ref[...]

    hp = state_ref.shape[0]

    @pl.when(t == 0)
    def _():
        for hh in range(hp):
            state_ref[hh] = s0_ref[0, hh].T if has_s0 else jnp.zeros((hd, hd), F32)

    outs, states = _hgrn_block(q_ref[...].astype(F32), z_ref[...].astype(F32), i_ref[...], lb_ref[...], tri,
                               [state_ref[hh] for hh in range(hp)], rev)
    for hh in range(hp):
        sl = slice(hh * hd, (hh + 1) * hd)
        state_ref[hh] = states[hh]
        if final:
            o = _rms(outs[hh] + of_ref[:, sl], onw_ref[...])
            o_ref[:, sl] = (o * _silu(gate_ref[:, sl].astype(F32))).astype(o_ref.dtype)
        else:
            o_ref[:, sl] = outs[hh]

    if emit_state:
        @pl.when(t == pl.num_programs(2) - 1)
        def _():
            for hh in range(hp):
                st_ref[0, hh] = state_ref[hh].T


def _hgrn_pass(proj, batch, seq, heads, lb, s0, *, rev, emit_state, prev=None, out_norm=None):
    hd = LANES
    tb = _pick_tile(seq, SCAN_BLOCK)
    assert tb == SCAN_BLOCK
    nblk = seq // tb
    hp = SCAN_HEADS
    assert heads % hp == 0
    ng = heads // hp
    final = prev is not None
    tmap = (lambda t: nblk - 1 - t) if rev else (lambda t: t)
    blk = lambda off: pl.BlockSpec((tb, hp * hd), lambda b, h, t: (b * nblk + tmap(t), off * ng + h))
    in_specs = [
        blk(0), blk(2 if rev else 1), blk(3),
        pl.BlockSpec((1, hp * hd), lambda b, h, t: (0, h)),
        pl.BlockSpec((tb, tb), lambda b, h, t: (0, 0)),
    ]
    args = [proj, proj, proj, lb.reshape(1, heads * hd), _scan_matrix(rev)]
    if s0 is not None:
        in_specs.append(pl.BlockSpec((1, hp, hd, hd), lambda b, h, t: (b, h, 0, 0)))
        args.append(s0)
    if final:
        in_specs += [blk(0), blk(4), pl.BlockSpec((1, hd), lambda b, h, t: (0, 0))]
        args += [prev, proj, out_norm.reshape(1, hd)]
    out_specs = [blk(0)]
    out_shape = [jax.ShapeDtypeStruct((batch * seq, heads * hd), BF16 if final else F32)]
    if emit_state:
        out_specs.append(pl.BlockSpec((1, hp, hd, hd), lambda b, h, t: (b, h, 0, 0)))
        out_shape.append(jax.ShapeDtypeStruct((batch, heads, hd, hd), F32))
    res = pl.pallas_call(
        functools.partial(_hgrn_kernel, rev=rev, has_s0=s0 is not None, emit_state=emit_state, final=final),
        grid=(batch, ng, nblk),
        in_specs=in_specs,
        out_specs=out_specs,
        out_shape=out_shape,
        scratch_shapes=[pltpu.VMEM((hp, hd, hd), F32)],
        compiler_params=_cparams(3),
        name="hgrn_scan_bwd" if rev else "hgrn_scan_fwd",
    )(*args)
    return res if emit_state else (res[0], None)


_ROPE_SWAP = np.concatenate([np.arange(16, 32), np.arange(0, 16), np.arange(48, 64), np.arange(32, 48)])


def _rope_table(seq, rotate):
    if not rotate:
        row = np.concatenate([np.ones(ROPE_D), np.zeros(ROPE_D)]).astype(np.float32)
        return jnp.asarray(np.broadcast_to(row, (seq, 2 * ROPE_D)))
    t = jnp.arange(seq)
    half = ROPE_D // 4
    inv = ROPE_BASE ** (-jnp.arange(half, dtype=F32) / half)
    ang_r = (t // GRID_W).astype(F32)[:, None] * inv[None, :]
    ang_c = (t % GRID_W).astype(F32)[:, None] * inv[None, :]
    cos = jnp.concatenate([jnp.cos(ang_r)] * 2 + [jnp.cos(ang_c)] * 2, axis=-1)
    sin = jnp.concatenate([-jnp.sin(ang_r), jnp.sin(ang_r), -jnp.sin(ang_c), jnp.sin(ang_c)], axis=-1)
    return jnp.concatenate([cos, sin], axis=-1)


def _rotate_rope(r, tab, lane_lo):
    rot = r * tab
    rot = rot + pltpu.roll(rot, ROPE_D, 1)
    return jnp.where(lane_lo, rot, 0.0)


def _qup_kernel(cq_ref, nw_ref, w_ref, gn_ref, gr_ref, tab_ref, o_ref, h_ref):
    @pl.when(pl.program_id(1) == 0)
    def _():
        h_ref[...] = _rms(cq_ref[...].astype(F32), nw_ref[...]).astype(BF16)

    y = _dot(h_ref[...], w_ref[...])
    hw = NOPE_D + 2 * ROPE_D
    scale = (NOPE_D + ROPE_D) ** -0.5 * LOG2E
    lane_lo = lax.broadcasted_iota(jnp.int32, (1, LANES), 1) < ROPE_D
    tab = tab_ref[...] * gr_ref[...]
    for hh in range(y.shape[1] // hw):
        yn = y[:, hh * hw:hh * hw + NOPE_D]
        yr = y[:, hh * hw + NOPE_D:(hh + 1) * hw]
        o_ref[:, hh * hw:hh * hw + NOPE_D] = (_rms(yn, gn_ref[...]) * scale).astype(o_ref.dtype)
        ms = jnp.sum(jnp.where(lane_lo, yr * yr, 0.0), axis=-1, keepdims=True) * (1.0 / ROPE_D)
        qr = _rotate_rope(yr, tab, lane_lo) * (lax.rsqrt(ms + EPS) * scale)
        o_ref[:, hh * hw + NOPE_D:(hh + 1) * hw] = qr.astype(o_ref.dtype)


def _q_up(proj, col_blk, q_lora, seq, a_norm, w, gain_n, gain_r, table):
    m = proj.shape[0]
    n = w.shape[1]
    tm = _pick_tile(seq, ROW_TILE)
    tn = _pick_tile(n, COL_TILE)
    nt = table.shape[0] // tm
    return pl.pallas_call(
        _qup_kernel,
        grid=(m // tm, n // tn),
        in_specs=[
            pl.BlockSpec((tm, q_lora), lambda i, j: (i, col_blk)),
            pl.BlockSpec((1, q_lora), lambda i, j: (0, 0)),
            pl.BlockSpec((q_lora, tn), lambda i, j: (0, j)),
            pl.BlockSpec((1, NOPE_D), lambda i, j: (0, 0)),
            pl.BlockSpec((1, 2 * ROPE_D), lambda i, j: (0, 0)),
            pl.BlockSpec((tm, 2 * ROPE_D), lambda i, j: (i % nt, 0)),
        ],
        out_specs=pl.BlockSpec((tm, tn), lambda i, j: (i, j)),
        out_shape=jax.ShapeDtypeStruct((m, n), BF16),
        scratch_shapes=[pltpu.VMEM((tm, q_lora), BF16)],
        compiler_params=_cparams(2),
        name="mla_q_up",
    )(proj, a_norm.reshape(1, q_lora), w, gain_n.reshape(1, NOPE_D), gain_r.reshape(1, 2 * ROPE_D), table)


def _kvup_kernel(*refs, normalize_in, emit_cache, heads):
    refs = list(refs)
    ckv_ref, kr_ref, nw_ref, gkr_ref, w_ref, gk_ref, tab_ref, k_ref, v_ref = refs[:9]
    lane_lo = lax.broadcasted_iota(jnp.int32, (1, LANES), 1) < ROPE_D
    c = ckv_ref[...].astype(F32)
    r = kr_ref[...].astype(F32)
    if normalize_in:
        c = _rms(c, nw_ref[...])
        ms = jnp.sum(jnp.where(lane_lo, r * r, 0.0), axis=-1, keepdims=True) * (1.0 / ROPE_D)
        r = r * lax.rsqrt(ms + EPS) * gkr_ref[...]
    if emit_cache:
        refs[9][...] = c
        refs[10][...] = r[:, :ROPE_D]
    k_rope = _rotate_rope(r, tab_ref[...], lane_lo).astype(k_ref.dtype)
    y = _dot(c.astype(BF16), w_ref[...])
    hw = 2 * LANES
    for h in range(heads):
        k_ref[:, h * hw:h * hw + NOPE_D] = _rms(y[:, h * hw:h * hw + NOPE_D], gk_ref[...]).astype(k_ref.dtype)
        k_ref[:, h * hw + NOPE_D:(h + 1) * hw] = k_rope
        v_ref[:, h * LANES:(h + 1) * LANES] = y[:, h * hw + NOPE_D:(h + 1) * hw].astype(v_ref.dtype)


def _kv_up(ckv_src, ckv_blk, kr_src, kr_blk, seq, kv_lora, heads, a_norm, gain_kr, w, gain_k, table,
           *, normalize_in, emit_cache):
    m = ckv_src.shape[0]
    tm = _pick_tile(seq, ROW_TILE)
    nt = table.shape[0] // tm
    out_specs = [
        pl.BlockSpec((tm, heads * 2 * LANES), lambda i: (i, 0)),
        pl.BlockSpec((tm, heads * LANES), lambda i: (i, 0)),
    ]
    out_shape = [
        jax.ShapeDtypeStruct((m, heads * 2 * LANES), BF16),
        jax.ShapeDtypeStruct((m, heads * LANES), BF16),
    ]
    if emit_cache:
        out_specs += [pl.BlockSpec((tm, kv_lora), lambda i: (i, 0)), pl.BlockSpec((tm, ROPE_D), lambda i: (i, 0))]
        out_shape += [jax.ShapeDtypeStruct((m, kv_lora), F32), jax.ShapeDtypeStruct((m, ROPE_D), F32)]
    return pl.pallas_call(
        functools.partial(_kvup_kernel, normalize_in=normalize_in, emit_cache=emit_cache, heads=heads),
        grid=(m // tm,),
        in_specs=[
            pl.BlockSpec((tm, kv_lora), lambda i: (i, ckv_blk)),
            pl.BlockSpec((tm, 2 * ROPE_D), lambda i: (i, kr_blk)),
            pl.BlockSpec((1, kv_lora), lambda i: (0, 0)),
            pl.BlockSpec((1, 2 * ROPE_D), lambda i: (0, 0)),
            pl.BlockSpec((kv_lora, heads * 2 * LANES), lambda i: (0, 0)),
            pl.BlockSpec((1, NOPE_D), lambda i: (0, 0)),
            pl.BlockSpec((tm, 2 * ROPE_D), lambda i: (i % nt, 0)),
        ],
        out_specs=out_specs,
        out_shape=out_shape,
        compiler_params=_cparams(1),
        name="mla_kv_up",
    )(ckv_src, kr_src, a_norm.reshape(1, kv_lora), gain_kr.reshape(1, 2 * ROPE_D), w,
      gain_k.reshape(1, NOPE_D), table)


def _mla_attn_kernel(*refs, has_ctx):
    if has_ctx:
        q_ref, k_ref, v_ref, kc_ref, vc_ref, g_ref, o_ref = refs
    else:
        q_ref, k_ref, v_ref, g_ref, o_ref = refs
    qw = 2 * LANES
    hp = q_ref.shape[-1] // qw
    scores = []
    for hh in range(hp):
        q = q_ref[:, hh * qw:(hh + 1) * qw]
        s = _dot_nt(q, k_ref[:, hh * qw:(hh + 1) * qw])
        s_c = _dot_nt(q, kc_ref[:, hh * qw:(hh + 1) * qw]) if has_ctx else None
        scores.append((s, s_c))
    for hh, (s, s_c) in enumerate(scores):
        sl = slice(hh * LANES, (hh + 1) * LANES)
        mx = jnp.max(s, axis=-1, keepdims=True)
        if has_ctx:
            mx = jnp.maximum(mx, jnp.max(s_c, axis=-1, keepdims=True))
        p = jnp.exp2(s - mx)
        l = jnp.sum(p, axis=-1, keepdims=True)
        acc = _dot(p.astype(BF16), v_ref[:, sl])
        if has_ctx:
            p_c = jnp.exp2(s_c - mx)
            l = l + jnp.sum(p_c, axis=-1, keepdims=True)
            acc = acc + _dot(p_c.astype(BF16), vc_ref[:, sl])
        o_ref[:, sl] = (acc / l * _silu(g_ref[:, sl].astype(F32))).astype(o_ref.dtype)


def _mla_attention(q, k, v, proj, gate_blk, batch, seq, heads, hp, ctx=None):
    tq = _pick_tile(seq, ATTN_Q)
    nq = seq // tq
    assert heads % hp == 0 and gate_blk % hp == 0
    qmap = lambda b, h, t: (b * nq + t, h)
    in_specs = [
        pl.BlockSpec((tq, hp * 2 * LANES), qmap),
        pl.BlockSpec((seq, hp * 2 * LANES), lambda b, h, t: (b, h)),
        pl.BlockSpec((seq, hp * LANES), lambda b, h, t: (b, h)),
    ]
    args = [q, k, v]
    if ctx is not None:
        k_c, v_c = ctx
        lc = k_c.shape[0] // batch
        in_specs += [
            pl.BlockSpec((lc, hp * 2 * LANES), lambda b, h, t: (b, h)),
            pl.BlockSpec((lc, hp * LANES), lambda b, h, t: (b, h)),
        ]
        args += [k_c, v_c]
    in_specs.append(pl.BlockSpec((tq, hp * LANES), lambda b, h, t: (b * nq + t, gate_blk // hp + h)))
    args.append(proj)
    return pl.pallas_call(
        functools.partial(_mla_attn_kernel, has_ctx=ctx is not None),
        grid=(batch, heads // hp, nq),
        in_specs=in_specs,
        out_specs=pl.BlockSpec((tq, hp * LANES), qmap),
        out_shape=jax.ShapeDtypeStruct((batch * seq, heads * LANES), BF16),
        compiler_params=_cparams(3),
        name="mla_attention",
    )(*args)


def _pad_cols(w, mult):
    pad = (-w.shape[1]) % mult
    return w if pad == 0 else jnp.concatenate([w, jnp.zeros((w.shape[0], pad), w.dtype)], axis=1)


def kernel(x_prompt, x_sample, cache_na_k, cache_na_v, state_hgrn, cache_mla_ckv, cache_mla_krope, c, c_ctx,
           norm_w, w_ada, b_ada, w_out,
           w_in_ab, na_q_norm, na_k_norm, na_rpb, sg_norm, sg_w, sg_b,
           w_in_cd, hgrn_lb, hgrn_out_norm, mla_q_a_norm, mla_w_q_up, mla_kv_a_norm, mla_w_kv_up,
           mla_q_norm, mla_k_norm):
    bp, lp, d = x_prompt.shape
    bs, ls, _ = x_sample.shape
    depth = norm_w.shape[0]
    hd = cache_na_k.shape[-1]
    heads = cache_na_k.shape[-2]
    width = heads * hd
    assert hd == LANES and state_hgrn.shape[-2:] == (hd, hd) and ls % GRID_W == 0
    q_lora = mla_q_a_norm.shape[-1]
    kv_lora = mla_kv_a_norm.shape[-1]
    assert mla_q_norm.shape[-1] == NOPE_D + ROPE_D and cache_mla_krope.shape[-1] == ROPE_D
    assert mla_w_kv_up.shape[-1] == heads * 2 * LANES

    n_cond = bs + 1
    cond_rows = -(-n_cond // 8) * 8
    cond = jnp.concatenate([c, c_ctx[None, :], jnp.zeros((cond_rows - n_cond, d), F32)], axis=0)
    mod = _modulation(cond, w_ada, b_ada).reshape(depth, cond_rows, 3, 1, d)

    lb_soft = jax.nn.softmax(hgrn_lb.astype(F32), axis=0)
    lb_cum = jnp.cumsum(lb_soft, axis=0)
    lower_bounds = lb_cum - lb_cum[:1]

    xp = x_prompt.reshape(bp * lp, d)
    xs = x_sample.reshape(bs * ls, d)
    na_k_new, na_v_new, hgrn_new, ckv_new, kr_new = [], [], [], [], []
    for layer in range(depth):
        j = layer // 2
        shift_s, scale_s, gate_s = (mod[layer, :bs, i] for i in range(3))
        shift_p, scale_p, gate_p = (mod[layer, bs:bs + 1, i] for i in range(3))
        w_o = w_out[layer].astype(BF16)
        if layer % 2 == 0:
            w_in = _pad_cols(w_in_ab[j].astype(BF16), COL_TILE)
            proj_p = _in_proj(xp, lp, norm_w[layer], scale_p, shift_p, w_in)
            proj_s = _in_proj(xs, ls, norm_w[layer], scale_s, shift_s, w_in)
            mix_a_p, k_new, v_new = _attn_ctx0(proj_p, bp, lp, heads, na_q_norm[j], na_k_norm[j])
            na_k_new.append(k_new.reshape(bp, lp, heads, hd))
            na_v_new.append(v_new.reshape(bp, lp, heads, hd))
            bias = _na_bias_tables(na_rpb[j], ls // GRID_W)
            lc = cache_na_k.shape[2]
            mix_a_s = _na_latent(proj_s, bs, ls, heads,
                                 cache_na_k[:, j].reshape(bs, lc, width).astype(BF16),
                                 cache_na_v[:, j].reshape(bs, lc, width).astype(BF16),
                                 bias, na_q_norm[j], na_k_norm[j])
            mix_b_p = _spatial_gating(proj_p, width, sg_norm[j], sg_w[j], sg_b[j])
            mix_b_s = _spatial_gating(proj_s, width, sg_norm[j], sg_w[j], sg_b[j])
        else:
            w_cd = w_in_cd[j].astype(BF16)
            kr_src = 5 * width + q_lora + kv_lora
            w_kr = w_cd[:, kr_src:kr_src + ROPE_D]
            w_in = jnp.concatenate([w_cd[:, :5 * width], w_cd[:, kr_src + ROPE_D:], w_cd[:, 5 * width:kr_src],
                                    w_kr, w_kr[:, _ROPE_SWAP]], axis=1)
            w_in = _pad_cols(w_in, COL_TILE)
            assert (6 * width) % q_lora == 0 and (6 * width + q_lora) % kv_lora == 0
            cq_blk = (6 * width) // q_lora
            ckv_blk = (6 * width + q_lora) // kv_lora
            kr_blk = (6 * width + q_lora + kv_lora) // LANES
            gd_blk = (5 * width) // LANES
            proj_p = _in_proj(xp, lp, norm_w[layer], scale_p, shift_p, w_in)
            proj_s = _in_proj(xs, ls, norm_w[layer], scale_s, shift_s, w_in)

            lb_f, lb_b = lower_bounds[layer, 0], lower_bounds[layer, 1]
            of_p, st_f = _hgrn_pass(proj_p, bp, lp, heads, lb_f, None, rev=False, emit_state=True)
            mix_a_p, st_b = _hgrn_pass(proj_p, bp, lp, heads, lb_b, None, rev=True, emit_state=True,
                                       prev=of_p, out_norm=hgrn_out_norm[j])
            hgrn_new.append(jnp.stack([st_f, st_b], axis=1))
            of_s, _ = _hgrn_pass(proj_s, bs, ls, heads, lb_f, state_hgrn[:, j, 0], rev=False, emit_state=False)
            mix_a_s, _ = _hgrn_pass(proj_s, bs, ls, heads, lb_b, state_hgrn[:, j, 1], rev=True, emit_state=False,
                                    prev=of_s, out_norm=hgrn_out_norm[j])

            wq = mla_w_q_up[j].reshape(q_lora, heads, NOPE_D + ROPE_D)
            wq = jnp.concatenate([wq, wq[:, :, NOPE_D:][:, :, _ROPE_SWAP]], axis=-1)
            wq = wq.reshape(q_lora, heads * 2 * LANES).astype(BF16)
            wkv = mla_w_kv_up[j].astype(BF16)
            gq, gk = mla_q_norm[j], mla_k_norm[j]
            gq_r = jnp.concatenate([gq[NOPE_D:], gq[NOPE_D:][_ROPE_SWAP]])
            gk_r = jnp.concatenate([gk[NOPE_D:], gk[NOPE_D:][_ROPE_SWAP]])
            tab_p = _rope_table(_pick_tile(lp, ROW_TILE), False)
            tab_s = _rope_table(ls, True)
            tab_c = _rope_table(_pick_tile(cache_mla_ckv.shape[2], ROW_TILE), False)

            q_p = _q_up(proj_p, cq_blk, q_lora, lp, mla_q_a_norm[j], wq, gq[:NOPE_D], gq_r, tab_p)
            k_p, v_p, ckv_n, kr_n = _kv_up(proj_p, ckv_blk, proj_p, kr_blk, lp, kv_lora, heads, mla_kv_a_norm[j],
                                           gk_r, wkv, gk[:NOPE_D], tab_p, normalize_in=True, emit_cache=True)
            ckv_new.append(ckv_n.reshape(bp, lp, kv_lora))
            kr_new.append(kr_n.reshape(bp, lp, ROPE_D))
            mix_b_p = _mla_attention(q_p, k_p, v_p, proj_p, gd_blk, bp, lp, heads, CTX_ATTN_HEADS)

            q_s = _q_up(proj_s, cq_blk, q_lora, ls, mla_q_a_norm[j], wq, gq[:NOPE_D], gq_r, tab_s)
            k_s, v_s = _kv_up(proj_s, ckv_blk, proj_s, kr_blk, ls, kv_lora, heads, mla_kv_a_norm[j],
                              gk_r, wkv, gk[:NOPE_D], tab_s, normalize_in=True, emit_cache=False)
            lc = cache_mla_ckv.shape[2]
            kr_c = cache_mla_krope[:, j].reshape(bs * lc, ROPE_D)
            kr_c = jnp.concatenate([kr_c, jnp.zeros_like(kr_c)], axis=1)
            k_c, v_c = _kv_up(cache_mla_ckv[:, j].reshape(bs * lc, kv_lora), 0, kr_c, 0, lc, kv_lora, heads,
                              mla_kv_a_norm[j], gk_r, wkv, gk[:NOPE_D], tab_c, normalize_in=False, emit_cache=False)
            mix_b_s = _mla_attention(q_s, k_s, v_s, proj_s, gd_blk, bs, ls, heads, LAT_ATTN_HEADS, ctx=(k_c, v_c))
        xp = _out_proj(mix_a_p, mix_b_p, w_o, xp, lp, gate_p)
        xs = _out_proj(mix_a_s, mix_b_s, w_o, xs, ls, gate_s)
    return (xp.reshape(bp, lp, d), xs.reshape(bs, ls, d),
            jnp.stack(na_k_new, axis=1), jnp.stack(na_v_new, axis=1), jnp.stack(hgrn_new, axis=1),
            jnp.stack(ckv_new, axis=1), jnp.stack(kr_new, axis=1))
```

```python
import functools

import numpy as np
import jax
import jax.numpy as jnp
from jax import lax
from jax.experimental import pallas as pl
from jax.experimental.pallas import tpu as pltpu

F32 = jnp.float32
BF16 = jnp.bfloat16

GRID_W = 64
WIN_R = 8
WIN_C = 16
SG_CHUNK = 128
NOPE_D = 128
ROPE_D = 64
ROPE_BASE = 10000.0
EPS = 1e-6
NEG_INF = -1e30

LANES = 128
VMEM_LIMIT = 56 * 1024 * 1024

ROW_TILE = 512
MM_ROW_TILE = 1024
COL_TILE = 1024
NA_ROWS = 4
NA_SPAN = NA_ROWS + WIN_R
ATTN_Q = 256
SCAN_BLOCK = 256
SCAN_CHUNK = 32
SCAN_HEADS = 16
CTX_ATTN_HEADS = 4
LAT_ATTN_HEADS = 2
LOG2E = 1.4426950408889634
EXP_CLAMP = 80.0


def _cparams(n_axes):
    return pltpu.CompilerParams(
        dimension_semantics=("arbitrary",) * n_axes, vmem_limit_bytes=VMEM_LIMIT)


def _pick_tile(n, pref):
    t = pref
    while n % t:
        t //= 2
    assert t >= LANES or t == n, (n, pref)
    return t


def _rms(x, w):
    return x * lax.rsqrt(jnp.mean(x * x, axis=-1, keepdims=True) + EPS) * w


def _ones_rows(n):
    row = lax.broadcasted_iota(jnp.int32, (LANES, LANES), 0)
    return jnp.where(row < n, 1.0, 0.0).astype(BF16)


def _head_rms(x, w, ones, n=LANES):
    ms = _dot((x * x).astype(BF16), ones) * (1.0 / n)
    return x * lax.rsqrt(ms + EPS) * w


def _silu(x):
    return x * jax.nn.sigmoid(x)


def _gelu(x):
    return 0.5 * x * (1.0 + jnp.tanh(0.7978845608028654 * (x + 0.044715 * (x * x * x))))


def _dot(a, b):
    return jnp.dot(a, b, preferred_element_type=F32)


def _dot_nt(a, b):
    return lax.dot_general(a, b, (((1,), (1,)), ((), ())), preferred_element_type=F32)


def _dot_tn(a, b):
    return lax.dot_general(a, b, (((0,), (0,)), ((), ())), preferred_element_type=F32)


def _mod_kernel(c_ref, w_ref, b_ref, o_ref):
    a = _silu(c_ref[...]).astype(BF16)
    o_ref[0] = _dot(a, w_ref[0].astype(BF16)) + b_ref[0]


def _modulation(cond, w_ada, b_ada):
    depth, d, n = w_ada.shape
    r = cond.shape[0]
    tn = _pick_tile(n, 512)
    return pl.pallas_call(
        _mod_kernel,
        grid=(depth, n // tn),
        in_specs=[
            pl.BlockSpec((r, d), lambda l, j: (0, 0)),
            pl.BlockSpec((1, d, tn), lambda l, j: (l, 0, j)),
            pl.BlockSpec((1, 1, tn), lambda l, j: (l, 0, j)),
        ],
        out_specs=pl.BlockSpec((1, r, tn), lambda l, j: (l, 0, j)),
        out_shape=jax.ShapeDtypeStruct((depth, r, n), F32),
        compiler_params=_cparams(2),
        name="adaln_modulation",
    )(cond, w_ada, b_ada.reshape(depth, 1, n))


def _norm_mod_kernel(x_ref, nw_ref, sc_ref, sh_ref, o_ref):
    h = _rms(x_ref[...], nw_ref[...])
    o_ref[...] = (h * (1.0 + sc_ref[0]) + sh_ref[0]).astype(o_ref.dtype)


def _matmul_kernel(a_ref, w_ref, o_ref):
    o_ref[...] = _dot(a_ref[...], w_ref[...]).astype(o_ref.dtype)


def _in_proj(x2, seq, norm_w, scale, shift, w):
    m, d = x2.shape
    n = w.shape[1]
    nb = scale.shape[0]
    tr = _pick_tile(m, ROW_TILE)
    if nb == 1:
        cond_map = lambda i: (0, 0, 0)
    else:
        assert seq % tr == 0
        cond_map = lambda i: ((i * tr) // seq, 0, 0)
    h = pl.pallas_call(
        _norm_mod_kernel,
        grid=(m // tr,),
        in_specs=[
            pl.BlockSpec((tr, d), lambda i: (i, 0)),
            pl.BlockSpec((1, d), lambda i: (0, 0)),
            pl.BlockSpec((1, 1, d), cond_map),
            pl.BlockSpec((1, 1, d), cond_map),
        ],
        out_specs=pl.BlockSpec((tr, d), lambda i: (i, 0)),
        out_shape=jax.ShapeDtypeStruct((m, d), BF16),
        compiler_params=_cparams(1),
        name="norm_modulate",
    )(x2, norm_w.reshape(1, d), scale, shift)
    tm = _pick_tile(m, MM_ROW_TILE)
    tn = _pick_tile(n, COL_TILE)
    return pl.pallas_call(
        _matmul_kernel,
        grid=(m // tm, n // tn),
        in_specs=[
            pl.BlockSpec((tm, d), lambda i, j: (i, 0)),
            pl.BlockSpec((d, tn), lambda i, j: (0, j)),
        ],
        out_specs=pl.BlockSpec((tm, tn), lambda i, j: (i, j)),
        out_shape=jax.ShapeDtypeStruct((m, n), BF16),
        compiler_params=_cparams(2),
        name="in_proj",
    )(h, w)


def _out_kernel(a_ref, b_ref, wa_ref, wb_ref, x_ref, g_ref, o_ref):
    acc = _dot(a_ref[...], wa_ref[...]) + _dot(b_ref[...], wb_ref[...])
    o_ref[...] = x_ref[...] + g_ref[0] * acc


def _out_proj(mix_a, mix_b, w_out, x2, seq, gate):
    m, d = x2.shape
    wdt = mix_a.shape[1]
    nb = gate.shape[0]
    tm = _pick_tile(m, MM_ROW_TILE)
    tn = _pick_tile(d, COL_TILE)
    if nb == 1:
        cond_map = lambda i, j: (0, 0, j)
    else:
        assert seq % tm == 0
        cond_map = lambda i, j: ((i * tm) // seq, 0, j)
    return pl.pallas_call(
        _out_kernel,
        grid=(m // tm, d // tn),
        in_specs=[
            pl.BlockSpec((tm, wdt), lambda i, j: (i, 0)),
            pl.BlockSpec((tm, wdt), lambda i, j: (i, 0)),
            pl.BlockSpec((wdt, tn), lambda i, j: (0, j)),
            pl.BlockSpec((wdt, tn), lambda i, j: (1, j)),
            pl.BlockSpec((tm, tn), lambda i, j: (i, j)),
            pl.BlockSpec((1, 1, tn), cond_map),
        ],
        out_specs=pl.BlockSpec((tm, tn), lambda i, j: (i, j)),
        out_shape=jax.ShapeDtypeStruct((m, d), F32),
        compiler_params=_cparams(2),
        name="out_proj_residual",
    )(mix_a, mix_b, w_out, w_out, x2, gate)


def _attn_ctx0_kernel(q_ref, k_ref, v_ref, g_ref, qw_ref, kw_ref, o_ref, ko_ref, vo_ref):
    hd = LANES
    qscale = hd ** -0.5 * LOG2E
    scores = []
    for hh in range(q_ref.shape[-1] // hd):
        sl = slice(hh * hd, (hh + 1) * hd)
        qn = _rms(q_ref[:, sl].astype(F32), qw_ref[...]) * qscale
        kn = _rms(k_ref[:, sl].astype(F32), kw_ref[...])
        ko_ref[:, sl] = kn
        vo_ref[:, sl] = v_ref[:, sl].astype(F32)
        scores.append(_dot_nt(qn.astype(BF16), kn.astype(BF16)))
    for hh, s in enumerate(scores):
        sl = slice(hh * hd, (hh + 1) * hd)
        p = jnp.exp2(s - jnp.max(s, axis=-1, keepdims=True))
        l = jnp.sum(p, axis=-1, keepdims=True)
        o = _dot(p.astype(BF16), v_ref[:, sl]) / l
        o_ref[:, sl] = (o * _silu(g_ref[:, sl].astype(F32))).astype(o_ref.dtype)


def _attn_ctx0(proj, batch, seq, heads, q_norm, k_norm):
    hd = LANES
    m = batch * seq
    hp = CTX_ATTN_HEADS
    assert heads % hp == 0
    ng = heads // hp
    blk = lambda off: pl.BlockSpec((seq, hp * hd), lambda b, h: (b, off * ng + h))
    vec = pl.BlockSpec((1, hd), lambda b, h: (0, 0))
    return pl.pallas_call(
        _attn_ctx0_kernel,
        grid=(batch, ng),
        in_specs=[blk(0), blk(1), blk(2), blk(3), vec, vec],
        out_specs=[blk(0), blk(0), blk(0)],
        out_shape=[
            jax.ShapeDtypeStruct((m, heads * hd), BF16),
            jax.ShapeDtypeStruct((m, heads * hd), F32),
            jax.ShapeDtypeStruct((m, heads * hd), F32),
        ],
        compiler_params=_cparams(2),
        name="ctx_attention_l0",
    )(proj, proj, proj, proj, q_norm.reshape(1, hd), k_norm.reshape(1, hd))


def _na_bias_plan(rows):
    tiles = rows // NA_ROWS
    assert min(WIN_R, rows) == WIN_R and tiles >= 3
    t_var = np.array([0, 1, tiles - 1])
    ks = np.clip(NA_ROWS * t_var - WIN_R // 2, 0, rows - NA_SPAN)
    r = NA_ROWS * t_var[:, None] + np.arange(NA_ROWS)[None, :]
    rs = np.clip(r - WIN_R // 2, 0, rows - WIN_R)
    kr = ks[:, None] + np.arange(NA_SPAN)[None, :]
    row_ok = (kr[:, None, :] >= rs[:, :, None]) & (kr[:, None, :] < rs[:, :, None] + WIN_R)
    dr = kr[:, None, :] - r[:, :, None] + WIN_R - 1
    return np.where(row_ok, dr, -1)


def _na_bias_kernel(rpb_ref, o_ref, *, plan):
    h = pl.program_id(0)
    n_dr, n_dc = 2 * WIN_R - 1, 2 * WIN_C - 1
    qc = lax.broadcasted_iota(jnp.int32, (GRID_W, GRID_W), 0)
    kc = lax.broadcasted_iota(jnp.int32, (GRID_W, GRID_W), 1)
    cstart = jnp.clip(qc - WIN_C // 2, 0, GRID_W - WIN_C)
    col_ok = (kc >= cstart) & (kc < cstart + WIN_C)
    dc = jnp.clip(kc - qc + WIN_C - 1, 0, n_dc - 1)
    neg = jnp.full((GRID_W, GRID_W), NEG_INF, F32)
    slabs = {-1: neg}
    for dr in sorted(set(int(v) for v in plan.ravel()) - {-1}):
        acc = neg
        for j in range(n_dc):
            acc = jnp.where(col_ok & (dc == j), rpb_ref[(h * n_dr + dr) * n_dc + j] * LOG2E, acc)
        slabs[dr] = acc
    for var in range(plan.shape[0]):
        for i in range(plan.shape[1]):
            row = jnp.concatenate([slabs[int(v)] for v in plan[var, i]], axis=1)
            o_ref[0, var, i * GRID_W:(i + 1) * GRID_W, :] = row


def _na_bias_tables(rpb, rows):
    heads = rpb.shape[0]
    shape = (heads, 3, NA_ROWS * GRID_W, NA_SPAN * GRID_W)
    return pl.pallas_call(
        functools.partial(_na_bias_kernel, plan=_na_bias_plan(rows)),
        grid=(heads,),
        in_specs=[pl.BlockSpec(memory_space=pltpu.SMEM)],
        out_specs=pl.BlockSpec((1,) + shape[1:], lambda h: (h, 0, 0, 0)),
        out_shape=jax.ShapeDtypeStruct(shape, F32),
        compiler_params=_cparams(1),
        name="na_bias_tables",
    )(rpb.astype(F32).reshape(-1))


def _na_kernel(q_ref, k_ref, v_ref, g_ref, kc_ref, vc_ref, b_ref, qw_ref, kw_ref, o_ref, kn_ref, *, rows):
    t = pl.program_id(2)
    hd = LANES
    hp = q_ref.shape[-1] // hd

    @pl.when(t == 0)
    def _():
        for hh in range(hp):
            sl = slice(hh * hd, (hh + 1) * hd)
            kn_ref[:, sl] = _rms(k_ref[:, sl].astype(F32), kw_ref[...]).astype(BF16)

    ks = jnp.clip(NA_ROWS * t - WIN_R // 2, 0, rows - NA_SPAN)
    span = pl.ds(pl.multiple_of(ks * GRID_W, GRID_W), NA_SPAN * GRID_W)
    qscale = hd ** -0.5 * LOG2E
    scores = []
    for hh in range(hp):
        sl = slice(hh * hd, (hh + 1) * hd)
        qn = (_rms(q_ref[:, sl].astype(F32), qw_ref[...]) * qscale).astype(BF16)
        scores.append((_dot_nt(qn, kn_ref[span, sl]) + b_ref[hh, 0], _dot_nt(qn, kc_ref[0, :, sl])))
    for hh, (s_lat, s_ctx) in enumerate(scores):
        sl = slice(hh * hd, (hh + 1) * hd)
        mx = jnp.maximum(jnp.max(s_lat, axis=-1, keepdims=True), jnp.max(s_ctx, axis=-1, keepdims=True))
        p_lat = jnp.exp2(s_lat - mx)
        p_ctx = jnp.exp2(s_ctx - mx)
        l = jnp.sum(p_lat, axis=-1, keepdims=True) + jnp.sum(p_ctx, axis=-1, keepdims=True)
        o = (_dot(p_lat.astype(BF16), v_ref[span, sl]) + _dot(p_ctx.astype(BF16), vc_ref[0, :, sl])) / l
        o_ref[:, sl] = (o * _silu(g_ref[:, sl].astype(F32))).astype(o_ref.dtype)


def _na_latent(proj, batch, seq, heads, k_ctx, v_ctx, bias, q_norm, k_norm):
    hd = LANES
    rows = seq // GRID_W
    tiles = rows // NA_ROWS
    tq = NA_ROWS * GRID_W
    lc = k_ctx.shape[1]
    hp = LAT_ATTN_HEADS
    assert heads % hp == 0
    ng = heads // hp
    qblk = lambda off: pl.BlockSpec((tq, hp * hd), lambda b, h, t: (b * tiles + t, off * ng + h))
    full = lambda off: pl.BlockSpec((seq, hp * hd), lambda b, h, t: (b, off * ng + h))
    ctx = pl.BlockSpec((1, lc, hp * hd), lambda b, h, t: (b, 0, h))
    variant = lambda t: jnp.where(t == 0, 0, jnp.where(t == tiles - 1, 2, 1))
    vec = pl.BlockSpec((1, hd), lambda b, h, t: (0, 0))
    return pl.pallas_call(
        functools.partial(_na_kernel, rows=rows),
        grid=(batch, ng, tiles),
        in_specs=[
            qblk(0), full(1), full(2), qblk(3), ctx, ctx,
            pl.BlockSpec((hp, 1, tq, NA_SPAN * GRID_W), lambda b, h, t: (h, variant(t), 0, 0)),
            vec, vec,
        ],
        out_specs=qblk(0),
        out_shape=jax.ShapeDtypeStruct((batch * seq, heads * hd), BF16),
        scratch_shapes=[pltpu.VMEM((seq, hp * hd), BF16)],
        compiler_params=_cparams(3),
        name="neighbourhood_attention",
    )(proj, proj, proj, proj, k_ctx, v_ctx, bias, q_norm.reshape(1, hd), k_norm.reshape(1, hd))


def _sg_kernel(u_ref, v_ref, g_ref, nw_ref, w_ref, b_ref, o_ref):
    groups = w_ref.shape[0]
    hd = w_ref.shape[-1]
    u = _gelu(u_ref[...].astype(F32))
    v = _rms(_gelu(v_ref[...].astype(F32)), nw_ref[...]).astype(BF16)
    g = _silu(g_ref[...].astype(F32))
    for gi in range(groups):
        sl = slice(gi * hd, (gi + 1) * hd)
        mixed = _dot(w_ref[gi], v[:, sl]) + b_ref[gi]
        o_ref[:, sl] = (u[:, sl] * mixed * g[:, sl]).astype(o_ref.dtype)


def _spatial_gating(proj, width, sg_norm, sg_w, sg_b):
    m = proj.shape[0]
    groups = sg_w.shape[0]
    blk = lambda off: pl.BlockSpec((SG_CHUNK, width), lambda i: (i, off))
    bias = jnp.broadcast_to(sg_b.astype(F32)[:, :, None], (groups, SG_CHUNK, LANES))
    return pl.pallas_call(
        _sg_kernel,
        grid=(m // SG_CHUNK,),
        in_specs=[
            blk(4), blk(5), blk(6),
            pl.BlockSpec((1, width), lambda i: (0, 0)),
            pl.BlockSpec((groups, SG_CHUNK, SG_CHUNK), lambda i: (0, 0, 0)),
            pl.BlockSpec((groups, SG_CHUNK, LANES), lambda i: (0, 0, 0)),
        ],
        out_specs=pl.BlockSpec((SG_CHUNK, width), lambda i: (i, 0)),
        out_shape=jax.ShapeDtypeStruct((m, width), BF16),
        compiler_params=_cparams(1),
        name="spatial_gating",
    )(proj, proj, proj, sg_norm.reshape(1, width), sg_w.astype(BF16), bias)


def _scan_matrix(rev):
    t = np.arange(SCAN_BLOCK)
    same = (t[:, None] // SCAN_CHUNK) == (t[None, :] // SCAN_CHUNK)
    incl = same & ((t[None, :] >= t[:, None]) if rev else (t[None, :] <= t[:, None]))
    return jnp.asarray(incl, dtype=BF16)


def _chunk_rows(a, pos):
    parts = []
    for c in range(a.shape[0] // SCAN_CHUNK):
        row = a[c * SCAN_CHUNK + pos:c * SCAN_CHUNK + pos + 1]
        parts.append(jnp.broadcast_to(row, (SCAN_CHUNK, a.shape[1])))
    return jnp.concatenate(parts, axis=0)


def _hgrn_block(q, z, v, lb, tri, states, rev):
    hd = LANES
    nc = q.shape[0] // SCAN_CHUNK
    half = SCAN_CHUNK // 2
    f = lb + (1.0 - lb) * jax.nn.sigmoid(z)
    g = jnp.log(f)
    k = 1.0 - f
    g_hi = g.astype(BF16)
    g_lo = (g - g_hi.astype(F32)).astype(BF16)
    a = _dot(tri, g_hi) + _dot(tri, g_lo)
    tot_row = 0 if rev else SCAN_CHUNK - 1
    a_ref = _chunk_rows(a, half if rev else half - 1)
    a_tot = _chunk_rows(a, tot_row)
    qe = (q * jnp.exp(jnp.minimum(a - a_ref, EXP_CLAMP))).astype(BF16)
    ke = (k * jnp.exp(jnp.minimum(a_ref - a, EXP_CLAMP))).astype(BF16)
    qa = (q * jnp.exp(a)).astype(BF16)
    kd = (k * jnp.exp(a_tot - a)).astype(BF16)
    heads = [slice(h * hd, (h + 1) * hd) for h in range(len(states))]
    chunks = [slice(c * SCAN_CHUNK, (c + 1) * SCAN_CHUNK) for c in range(nc)]
    dec = [jnp.exp(a[c * SCAN_CHUNK + tot_row:c * SCAN_CHUNK + tot_row + 1]) for c in range(nc)]
    probs = [_dot_nt(qe[:, sl], ke[:, sl]) for sl in heads]
    upd = [[_dot_tn(v[cs, sl], kd[cs, sl]) for cs in chunks] for sl in heads]
    in_chunk = tri > 0
    intra = [_dot(jnp.where(in_chunk, p.astype(BF16), 0.0), v[:, sl]) for p, sl in zip(probs, heads)]
    outs, new_states = [], []
    for h, sl in enumerate(heads):
        state = states[h]
        starts = [None] * nc
        for c in (range(nc - 1, -1, -1) if rev else range(nc)):
            starts[c] = state.astype(BF16)
            state = state * dec[c][:, sl] + upd[h][c]
        inter = [_dot_nt(qa[cs, sl], starts[c]) for c, cs in enumerate(chunks)]
        outs.append(intra[h] + jnp.concatenate(inter, axis=0))
        new_states.append(state)
    return outs, new_states


def _hgrn_kernel(*refs, rev, has_s0, emit_state, final):
    refs = list(refs)
    q_ref, z_ref, i_ref, lb_ref, tri_ref = refs[:5]
    del refs[:5]
    s0_ref = refs.pop(0) if has_s0 else None
    if final:
        of_ref, gate_ref, onw_ref = refs[:3]
        del refs[:3]
    o_ref = refs.pop(0)
    st_ref = refs.pop(0) if emit_state else None
    state_ref = refs.pop(0)
    t = pl.program_id(2)
    hd = LANES
    tri = tri_ref[...]

    hp = state_ref.shape[0]

    @pl.when(t == 0)
    def _():
        for hh in range(hp):
            state_ref[hh] = s0_ref[0, hh].T if has_s0 else jnp.zeros((hd, hd), F32)

    outs, states = _hgrn_block(q_ref[...].astype(F32), z_ref[...].astype(F32), i_ref[...], lb_ref[...], tri,
                               [state_ref[hh] for hh in range(hp)], rev)
    for hh in range(hp):
        sl = slice(hh * hd, (hh + 1) * hd)
        state_ref[hh] = states[hh]
        if final:
            o = _rms(outs[hh] + of_ref[:, sl], onw_ref[...])
            o_ref[:, sl] = (o * _silu(gate_ref[:, sl].astype(F32))).astype(o_ref.dtype)
        else:
            o_ref[:, sl] = outs[hh]

    if emit_state:
        @pl.when(t == pl.num_programs(2) - 1)
        def _():
            for hh in range(hp):
                st_ref[0, hh] = state_ref[hh].T


def _hgrn_pass(proj, batch, seq, heads, lb, s0, *, rev, emit_state, prev=None, out_norm=None):
    hd = LANES
    tb = _pick_tile(seq, SCAN_BLOCK)
    assert tb == SCAN_BLOCK
    nblk = seq // tb
    hp = min(SCAN_HEADS, heads)
    assert heads % hp == 0
    ng = heads // hp
    final = prev is not None
    tmap = (lambda t: nblk - 1 - t) if rev else (lambda t: t)
    blk = lambda off: pl.BlockSpec((tb, hp * hd), lambda b, h, t: (b * nblk + tmap(t), off * ng + h))
    in_specs = [
        blk(0), blk(2 if rev else 1), blk(3),
        pl.BlockSpec((1, hp * hd), lambda b, h, t: (0, h)),
        pl.BlockSpec((tb, tb), lambda b, h, t: (0, 0)),
    ]
    args = [proj, proj, proj, lb.reshape(1, heads * hd), _scan_matrix(rev)]
    if s0 is not None:
        in_specs.append(pl.BlockSpec((1, hp, hd, hd), lambda b, h, t: (b, h, 0, 0)))
        args.append(s0)
    if final:
        in_specs += [blk(0), blk(4), pl.BlockSpec((1, hd), lambda b, h, t: (0, 0))]
        args += [prev, proj, out_norm.reshape(1, hd)]
    out_specs = [blk(0)]
    out_shape = [jax.ShapeDtypeStruct((batch * seq, heads * hd), BF16 if final else F32)]
    if emit_state:
        out_specs.append(pl.BlockSpec((1, hp, hd, hd), lambda b, h, t: (b, h, 0, 0)))
        out_shape.append(jax.ShapeDtypeStruct((batch, heads, hd, hd), F32))
    res = pl.pallas_call(
        functools.partial(_hgrn_kernel, rev=rev, has_s0=s0 is not None, emit_state=emit_state, final=final),
        grid=(batch, ng, nblk),
        in_specs=in_specs,
        out_specs=out_specs,
        out_shape=out_shape,
        scratch_shapes=[pltpu.VMEM((hp, hd, hd), F32)],
        compiler_params=_cparams(3),
        name="hgrn_scan_bwd" if rev else "hgrn_scan_fwd",
    )(*args)
    return res if emit_state else (res[0], None)


_ROPE_SWAP = np.concatenate([np.arange(16, 32), np.arange(0, 16), np.arange(48, 64), np.arange(32, 48)])


def _rope_table(seq, rotate):
    if not rotate:
        row = np.concatenate([np.ones(ROPE_D), np.zeros(ROPE_D)]).astype(np.float32)
        return jnp.asarray(np.broadcast_to(row, (seq, 2 * ROPE_D)))
    t = jnp.arange(seq)
    half = ROPE_D // 4
    inv = ROPE_BASE ** (-jnp.arange(half, dtype=F32) / half)
    ang_r = (t // GRID_W).astype(F32)[:, None] * inv[None, :]
    ang_c = (t % GRID_W).astype(F32)[:, None] * inv[None, :]
    cos = jnp.concatenate([jnp.cos(ang_r)] * 2 + [jnp.cos(ang_c)] * 2, axis=-1)
    sin = jnp.concatenate([-jnp.sin(ang_r), jnp.sin(ang_r), -jnp.sin(ang_c), jnp.sin(ang_c)], axis=-1)
    return jnp.concatenate([cos, sin], axis=-1)


def _rotate_rope(r, tab, lane_lo):
    rot = r * tab
    rot = rot + pltpu.roll(rot, ROPE_D, 1)
    return jnp.where(lane_lo, rot, 0.0)


def _qup_kernel(cq_ref, nw_ref, w_ref, gn_ref, gr_ref, tab_ref, o_ref, h_ref):
    @pl.when(pl.program_id(1) == 0)
    def _():
        h_ref[...] = _rms(cq_ref[...].astype(F32), nw_ref[...]).astype(BF16)

    hw = NOPE_D + 2 * ROPE_D
    scale = (NOPE_D + ROPE_D) ** -0.5 * LOG2E
    lane_lo = lax.broadcasted_iota(jnp.int32, (1, LANES), 1) < ROPE_D
    tab = tab_ref[...] * gr_ref[...]
    y = _dot(h_ref[...], w_ref[...])
    ones_all = _ones_rows(LANES)
    ones_lo = _ones_rows(ROPE_D)
    gain_n = gn_ref[...] * scale
    for hh in range(y.shape[1] // hw):
        yn = y[:, hh * hw:hh * hw + NOPE_D]
        yr = y[:, hh * hw + NOPE_D:(hh + 1) * hw]
        o_ref[:, hh * hw:hh * hw + NOPE_D] = _head_rms(yn, gain_n, ones_all).astype(o_ref.dtype)
        ms_r = _dot((yr * yr).astype(BF16), ones_lo) * (1.0 / ROPE_D)
        qr = _rotate_rope(yr, tab, lane_lo) * (lax.rsqrt(ms_r + EPS) * scale)
        o_ref[:, hh * hw + NOPE_D:(hh + 1) * hw] = qr.astype(o_ref.dtype)


def _q_up(proj, col_blk, q_lora, seq, a_norm, w, gain_n, gain_r, table):
    m = proj.shape[0]
    n = w.shape[1]
    tm = _pick_tile(seq, ROW_TILE)
    tn = _pick_tile(n, COL_TILE)
    nt = table.shape[0] // tm
    return pl.pallas_call(
        _qup_kernel,
        grid=(m // tm, n // tn),
        in_specs=[
            pl.BlockSpec((tm, q_lora), lambda i, j: (i, col_blk)),
            pl.BlockSpec((1, q_lora), lambda i, j: (0, 0)),
            pl.BlockSpec((q_lora, tn), lambda i, j: (0, j)),
            pl.BlockSpec((1, NOPE_D), lambda i, j: (0, 0)),
            pl.BlockSpec((1, 2 * ROPE_D), lambda i, j: (0, 0)),
            pl.BlockSpec((tm, 2 * ROPE_D), lambda i, j: (i % nt, 0)),
        ],
        out_specs=pl.BlockSpec((tm, tn), lambda i, j: (i, j)),
        out_shape=jax.ShapeDtypeStruct((m, n), BF16),
        scratch_shapes=[pltpu.VMEM((tm, q_lora), BF16)],
        compiler_params=_cparams(2),
        name="mla_q_up",
    )(proj, a_norm.reshape(1, q_lora), w, gain_n.reshape(1, NOPE_D), gain_r.reshape(1, 2 * ROPE_D), table)


def _kvup_kernel(*refs, normalize_in, emit_cache, heads):
    refs = list(refs)
    ckv_ref, kr_ref, nw_ref, gkr_ref, w_ref, gk_ref, tab_ref, k_ref, v_ref = refs[:9]
    lane_lo = lax.broadcasted_iota(jnp.int32, (1, LANES), 1) < ROPE_D
    c = ckv_ref[...].astype(F32)
    r = kr_ref[...].astype(F32)
    if normalize_in:
        c = _rms(c, nw_ref[...])
        ms = jnp.sum(jnp.where(lane_lo, r * r, 0.0), axis=-1, keepdims=True) * (1.0 / ROPE_D)
        r = r * lax.rsqrt(ms + EPS) * gkr_ref[...]
    if emit_cache:
        refs[9][...] = c
        refs[10][...] = r[:, :ROPE_D]
    k_rope = _rotate_rope(r, tab_ref[...], lane_lo).astype(k_ref.dtype)
    y = _dot(c.astype(BF16), w_ref[...])
    hw = 2 * LANES
    for h in range(heads):
        k_ref[:, h * hw:h * hw + NOPE_D] = _rms(y[:, h * hw:h * hw + NOPE_D], gk_ref[...]).astype(k_ref.dtype)
        k_ref[:, h * hw + NOPE_D:(h + 1) * hw] = k_rope
        v_ref[:, h * LANES:(h + 1) * LANES] = y[:, h * hw + NOPE_D:(h + 1) * hw].astype(v_ref.dtype)


def _kv_up(ckv_src, ckv_blk, kr_src, kr_blk, seq, kv_lora, heads, a_norm, gain_kr, w, gain_k, table,
           *, normalize_in, emit_cache):
    m = ckv_src.shape[0]
    tm = _pick_tile(seq, ROW_TILE)
    nt = table.shape[0] // tm
    out_specs = [
        pl.BlockSpec((tm, heads * 2 * LANES), lambda i: (i, 0)),
        pl.BlockSpec((tm, heads * LANES), lambda i: (i, 0)),
    ]
    out_shape = [
        jax.ShapeDtypeStruct((m, heads * 2 * LANES), BF16),
        jax.ShapeDtypeStruct((m, heads * LANES), BF16),
    ]
    if emit_cache:
        out_specs += [pl.BlockSpec((tm, kv_lora), lambda i: (i, 0)), pl.BlockSpec((tm, ROPE_D), lambda i: (i, 0))]
        out_shape += [jax.ShapeDtypeStruct((m, kv_lora), F32), jax.ShapeDtypeStruct((m, ROPE_D), F32)]
    return pl.pallas_call(
        functools.partial(_kvup_kernel, normalize_in=normalize_in, emit_cache=emit_cache, heads=heads),
        grid=(m // tm,),
        in_specs=[
            pl.BlockSpec((tm, kv_lora), lambda i: (i, ckv_blk)),
            pl.BlockSpec((tm, 2 * ROPE_D), lambda i: (i, kr_blk)),
            pl.BlockSpec((1, kv_lora), lambda i: (0, 0)),
            pl.BlockSpec((1, 2 * ROPE_D), lambda i: (0, 0)),
            pl.BlockSpec((kv_lora, heads * 2 * LANES), lambda i: (0, 0)),
            pl.BlockSpec((1, NOPE_D), lambda i: (0, 0)),
            pl.BlockSpec((tm, 2 * ROPE_D), lambda i: (i % nt, 0)),
        ],
        out_specs=out_specs,
        out_shape=out_shape,
        compiler_params=_cparams(1),
        name="mla_kv_up",
    )(ckv_src, kr_src, a_norm.reshape(1, kv_lora), gain_kr.reshape(1, 2 * ROPE_D), w,
      gain_k.reshape(1, NOPE_D), table)


def _mla_attn_kernel(*refs, has_ctx):
    if has_ctx:
        q_ref, k_ref, v_ref, kc_ref, vc_ref, g_ref, o_ref = refs
    else:
        q_ref, k_ref, v_ref, g_ref, o_ref = refs
    qw = 2 * LANES
    hp = q_ref.shape[-1] // qw
    scores = []
    for hh in range(hp):
        q = q_ref[:, hh * qw:(hh + 1) * qw]
        s = _dot_nt(q, k_ref[:, hh * qw:(hh + 1) * qw])
        s_c = _dot_nt(q, kc_ref[:, hh * qw:(hh + 1) * qw]) if has_ctx else None
        scores.append((s, s_c))
    for hh, (s, s_c) in enumerate(scores):
        sl = slice(hh * LANES, (hh + 1) * LANES)
        mx = jnp.max(s, axis=-1, keepdims=True)
        if has_ctx:
            mx = jnp.maximum(mx, jnp.max(s_c, axis=-1, keepdims=True))
        p = jnp.exp2(s - mx)
        l = jnp.sum(p, axis=-1, keepdims=True)
        acc = _dot(p.astype(BF16), v_ref[:, sl])
        if has_ctx:
            p_c = jnp.exp2(s_c - mx)
            l = l + jnp.sum(p_c, axis=-1, keepdims=True)
            acc = acc + _dot(p_c.astype(BF16), vc_ref[:, sl])
        o_ref[:, sl] = (acc / l * _silu(g_ref[:, sl].astype(F32))).astype(o_ref.dtype)


def _mla_attention(q, k, v, proj, gate_blk, batch, seq, heads, hp, ctx=None):
    tq = _pick_tile(seq, ATTN_Q)
    nq = seq // tq
    assert heads % hp == 0 and gate_blk % hp == 0
    qmap = lambda b, h, t: (b * nq + t, h)
    in_specs = [
        pl.BlockSpec((tq, hp * 2 * LANES), qmap),
        pl.BlockSpec((seq, hp * 2 * LANES), lambda b, h, t: (b, h)),
        pl.BlockSpec((seq, hp * LANES), lambda b, h, t: (b, h)),
    ]
    args = [q, k, v]
    if ctx is not None:
        k_c, v_c = ctx
        lc = k_c.shape[0] // batch
        in_specs += [
            pl.BlockSpec((lc, hp * 2 * LANES), lambda b, h, t: (b, h)),
            pl.BlockSpec((lc, hp * LANES), lambda b, h, t: (b, h)),
        ]
        args += [k_c, v_c]
    in_specs.append(pl.BlockSpec((tq, hp * LANES), lambda b, h, t: (b * nq + t, gate_blk // hp + h)))
    args.append(proj)
    return pl.pallas_call(
        functools.partial(_mla_attn_kernel, has_ctx=ctx is not None),
        grid=(batch, heads // hp, nq),
        in_specs=in_specs,
        out_specs=pl.BlockSpec((tq, hp * LANES), qmap),
        out_shape=jax.ShapeDtypeStruct((batch * seq, heads * LANES), BF16),
        compiler_params=_cparams(3),
        name="mla_attention",
    )(*args)


def _pad_cols(w, mult):
    pad = (-w.shape[1]) % mult
    return w if pad == 0 else jnp.concatenate([w, jnp.zeros((w.shape[0], pad), w.dtype)], axis=1)


def kernel(x_prompt, x_sample, cache_na_k, cache_na_v, state_hgrn, cache_mla_ckv, cache_mla_krope, c, c_ctx,
           norm_w, w_ada, b_ada, w_out,
           w_in_ab, na_q_norm, na_k_norm, na_rpb, sg_norm, sg_w, sg_b,
           w_in_cd, hgrn_lb, hgrn_out_norm, mla_q_a_norm, mla_w_q_up, mla_kv_a_norm, mla_w_kv_up,
           mla_q_norm, mla_k_norm):
    bp, lp, d = x_prompt.shape
    bs, ls, _ = x_sample.shape
    depth = norm_w.shape[0]
    hd = cache_na_k.shape[-1]
    heads = cache_na_k.shape[-2]
    width = heads * hd
    assert hd == LANES and state_hgrn.shape[-2:] == (hd, hd) and ls % GRID_W == 0
    q_lora = mla_q_a_norm.shape[-1]
    kv_lora = mla_kv_a_norm.shape[-1]
    assert mla_q_norm.shape[-1] == NOPE_D + ROPE_D and cache_mla_krope.shape[-1] == ROPE_D
    assert mla_w_kv_up.shape[-1] == heads * 2 * LANES

    n_cond = bs + 1
    cond_rows = -(-n_cond // 8) * 8
    cond = jnp.concatenate([c, c_ctx[None, :], jnp.zeros((cond_rows - n_cond, d), F32)], axis=0)
    mod = _modulation(cond, w_ada, b_ada).reshape(depth, cond_rows, 3, 1, d)

    lb_soft = jax.nn.softmax(hgrn_lb.astype(F32), axis=0)
    lb_cum = jnp.cumsum(lb_soft, axis=0)
    lower_bounds = lb_cum - lb_cum[:1]

    xp = x_prompt.reshape(bp * lp, d)
    xs = x_sample.reshape(bs * ls, d)
    na_k_new, na_v_new, hgrn_new, ckv_new, kr_new = [], [], [], [], []
    for layer in range(depth):
        j = layer // 2
        shift_s, scale_s, gate_s = (mod[layer, :bs, i] for i in range(3))
        shift_p, scale_p, gate_p = (mod[layer, bs:bs + 1, i] for i in range(3))
        w_o = w_out[layer].astype(BF16)
        if layer % 2 == 0:
            w_in = _pad_cols(w_in_ab[j].astype(BF16), COL_TILE)
            proj_p = _in_proj(xp, lp, norm_w[layer], scale_p, shift_p, w_in)
            proj_s = _in_proj(xs, ls, norm_w[layer], scale_s, shift_s, w_in)
            mix_a_p, k_new, v_new = _attn_ctx0(proj_p, bp, lp, heads, na_q_norm[j], na_k_norm[j])
            na_k_new.append(k_new.reshape(bp, lp, heads, hd))
            na_v_new.append(v_new.reshape(bp, lp, heads, hd))
            bias = _na_bias_tables(na_rpb[j], ls // GRID_W)
            lc = cache_na_k.shape[2]
            mix_a_s = _na_latent(proj_s, bs, ls, heads,
                                 cache_na_k[:, j].reshape(bs, lc, width).astype(BF16),
                                 cache_na_v[:, j].reshape(bs, lc, width).astype(BF16),
                                 bias, na_q_norm[j], na_k_norm[j])
            mix_b_p = _spatial_gating(proj_p, width, sg_norm[j], sg_w[j], sg_b[j])
            mix_b_s = _spatial_gating(proj_s, width, sg_norm[j], sg_w[j], sg_b[j])
        else:
            w_cd = w_in_cd[j]
            kr_src = 5 * width + q_lora + kv_lora
            w_kr = w_cd[:, kr_src:kr_src + ROPE_D]
            pieces = [w_cd[:, :5 * width], w_cd[:, kr_src + ROPE_D:], w_cd[:, 5 * width:kr_src],
                      w_kr, w_kr[:, _ROPE_SWAP]]
            n_cd = sum(p.shape[1] for p in pieces)
            pieces.append(jnp.zeros((d, (-n_cd) % COL_TILE), F32))
            w_in = jnp.concatenate([p.astype(BF16) for p in pieces], axis=1)
            assert (6 * width) % q_lora == 0 and (6 * width + q_lora) % kv_lora == 0
            cq_blk = (6 * width) // q_lora
            ckv_blk = (6 * width + q_lora) // kv_lora
            kr_blk = (6 * width + q_lora + kv_lora) // LANES
            gd_blk = (5 * width) // LANES
            proj_p = _in_proj(xp, lp, norm_w[layer], scale_p, shift_p, w_in)
            proj_s = _in_proj(xs, ls, norm_w[layer], scale_s, shift_s, w_in)

            lb_f, lb_b = lower_bounds[layer, 0], lower_bounds[layer, 1]
            of_p, st_f = _hgrn_pass(proj_p, bp, lp, heads, lb_f, None, rev=False, emit_state=True)
            mix_a_p, st_b = _hgrn_pass(proj_p, bp, lp, heads, lb_b, None, rev=True, emit_state=True,
                                       prev=of_p, out_norm=hgrn_out_norm[j])
            hgrn_new.append(jnp.stack([st_f, st_b], axis=1))
            of_s, _ = _hgrn_pass(proj_s, bs, ls, heads, lb_f, state_hgrn[:, j, 0], rev=False, emit_state=False)
            mix_a_s, _ = _hgrn_pass(proj_s, bs, ls, heads, lb_b, state_hgrn[:, j, 1], rev=True, emit_state=False,
                                    prev=of_s, out_norm=hgrn_out_norm[j])

            wq = mla_w_q_up[j].reshape(q_lora, heads, NOPE_D + ROPE_D)
            wq = jnp.concatenate([wq, wq[:, :, NOPE_D:][:, :, _ROPE_SWAP]], axis=-1)
            wq = wq.reshape(q_lora, heads * 2 * LANES).astype(BF16)
            wkv = mla_w_kv_up[j].astype(BF16)
            gq, gk = mla_q_norm[j], mla_k_norm[j]
            gq_r = jnp.concatenate([gq[NOPE_D:], gq[NOPE_D:][_ROPE_SWAP]])
            gk_r = jnp.concatenate([gk[NOPE_D:], gk[NOPE_D:][_ROPE_SWAP]])
            tab_p = _rope_table(_pick_tile(lp, ROW_TILE), False)
            tab_s = _rope_table(ls, True)
            tab_c = _rope_table(_pick_tile(cache_mla_ckv.shape[2], ROW_TILE), False)

            q_p = _q_up(proj_p, cq_blk, q_lora, lp, mla_q_a_norm[j], wq, gq[:NOPE_D], gq_r, tab_p)
            k_p, v_p, ckv_n, kr_n = _kv_up(proj_p, ckv_blk, proj_p, kr_blk, lp, kv_lora, heads, mla_kv_a_norm[j],
                                           gk_r, wkv, gk[:NOPE_D], tab_p, normalize_in=True, emit_cache=True)
            ckv_new.append(ckv_n.reshape(bp, lp, kv_lora))
            kr_new.append(kr_n.reshape(bp, lp, ROPE_D))
            mix_b_p = _mla_attention(q_p, k_p, v_p, proj_p, gd_blk, bp, lp, heads, CTX_ATTN_HEADS)

            q_s = _q_up(proj_s, cq_blk, q_lora, ls, mla_q_a_norm[j], wq, gq[:NOPE_D], gq_r, tab_s)
            k_s, v_s = _kv_up(proj_s, ckv_blk, proj_s, kr_blk, ls, kv_lora, heads, mla_kv_a_norm[j],
                              gk_r, wkv, gk[:NOPE_D], tab_s, normalize_in=True, emit_cache=False)
            lc = cache_mla_ckv.shape[2]
            kr_c = cache_mla_krope[:, j].reshape(bs * lc, ROPE_D)
            kr_c = jnp.concatenate([kr_c, jnp.zeros_like(kr_c)], axis=1)
            k_c, v_c = _kv_up(cache_mla_ckv[:, j].reshape(bs * lc, kv_lora), 0, kr_c, 0, lc, kv_lora, heads,
                              mla_kv_a_norm[j], gk_r, wkv, gk[:NOPE_D], tab_c, normalize_in=False, emit_cache=False)
            mix_b_s = _mla_attention(q_s, k_s, v_s, proj_s, gd_blk, bs, ls, heads, LAT_ATTN_HEADS, ctx=(k_c, v_c))
        xp = _out_proj(mix_a_p, mix_b_p, w_o, xp, lp, gate_p)
        xs = _out_proj(mix_a_s, mix_b_s, w_o, xs, ls, gate_s)
    return (xp.reshape(bp, lp, d), xs.reshape(bs, ls, d),
            jnp.stack(na_k_new, axis=1), jnp.stack(na_v_new, axis=1), jnp.stack(hgrn_new, axis=1),
            jnp.stack(ckv_new, axis=1), jnp.stack(kr_new, axis=1))
```

```python
import functools

import numpy as np
import jax
import jax.numpy as jnp
from jax import lax
from jax.experimental import pallas as pl
from jax.experimental.pallas import tpu as pltpu

F32 = jnp.float32
BF16 = jnp.bfloat16

GRID_W = 64
WIN_R = 8
WIN_C = 16
SG_CHUNK = 128
NOPE_D = 128
ROPE_D = 64
ROPE_BASE = 10000.0
EPS = 1e-6
NEG_INF = -1e30

LANES = 128
VMEM_LIMIT = 56 * 1024 * 1024

ROW_TILE = 512
MM_ROW_TILE = 1024
COL_TILE = 1024
NA_ROWS = 4
NA_SPAN = NA_ROWS + WIN_R
ATTN_Q = 256
SCAN_BLOCK = 256
SCAN_CHUNK = 32
SCAN_HEADS = 16
CTX_ATTN_HEADS = 4
LAT_ATTN_HEADS = 2
LOG2E = 1.4426950408889634
RELAYOUT_ROWS = 256
EXP_CLAMP = 80.0


def _cparams(n_axes):
    return pltpu.CompilerParams(
        dimension_semantics=("arbitrary",) * n_axes, vmem_limit_bytes=VMEM_LIMIT)


def _pick_tile(n, pref):
    t = pref
    while n % t:
        t //= 2
    assert t >= LANES or t == n, (n, pref)
    return t


def _rms(x, w):
    return x * lax.rsqrt(jnp.mean(x * x, axis=-1, keepdims=True) + EPS) * w


def _ones_rows(n):
    row = lax.broadcasted_iota(jnp.int32, (LANES, LANES), 0)
    return jnp.where(row < n, 1.0, 0.0).astype(BF16)


def _head_rms(x, w, ones, n=LANES):
    ms = _dot((x * x).astype(BF16), ones) * (1.0 / n)
    return x * lax.rsqrt(ms + EPS) * w


def _silu(x):
    return x * jax.nn.sigmoid(x)


def _gelu(x):
    return 0.5 * x * (1.0 + jnp.tanh(0.7978845608028654 * (x + 0.044715 * (x * x * x))))


def _dot(a, b):
    return jnp.dot(a, b, preferred_element_type=F32)


def _dot_nt(a, b):
    return lax.dot_general(a, b, (((1,), (1,)), ((), ())), preferred_element_type=F32)


def _dot_tn(a, b):
    return lax.dot_general(a, b, (((0,), (0,)), ((), ())), preferred_element_type=F32)


def _mod_kernel(c_ref, w_ref, b_ref, o_ref):
    a = _silu(c_ref[...]).astype(BF16)
    o_ref[0] = _dot(a, w_ref[0].astype(BF16)) + b_ref[0]


def _modulation(cond, w_ada, b_ada):
    depth, d, n = w_ada.shape
    r = cond.shape[0]
    tn = _pick_tile(n, 512)
    return pl.pallas_call(
        _mod_kernel,
        grid=(depth, n // tn),
        in_specs=[
            pl.BlockSpec((r, d), lambda l, j: (0, 0)),
            pl.BlockSpec((1, d, tn), lambda l, j: (l, 0, j)),
            pl.BlockSpec((1, 1, tn), lambda l, j: (l, 0, j)),
        ],
        out_specs=pl.BlockSpec((1, r, tn), lambda l, j: (l, 0, j)),
        out_shape=jax.ShapeDtypeStruct((depth, r, n), F32),
        compiler_params=_cparams(2),
        name="adaln_modulation",
    )(cond, w_ada, b_ada.reshape(depth, 1, n))


def _norm_mod_kernel(x_ref, nw_ref, sc_ref, sh_ref, o_ref):
    h = _rms(x_ref[...], nw_ref[...])
    o_ref[...] = (h * (1.0 + sc_ref[0]) + sh_ref[0]).astype(o_ref.dtype)


def _matmul_kernel(a_ref, w_ref, o_ref):
    o_ref[...] = _dot(a_ref[...], w_ref[...]).astype(o_ref.dtype)


def _in_proj(x2, seq, norm_w, scale, shift, w):
    m, d = x2.shape
    n = w.shape[1]
    nb = scale.shape[0]
    tr = _pick_tile(m, ROW_TILE)
    if nb == 1:
        cond_map = lambda i: (0, 0, 0)
    else:
        assert seq % tr == 0
        cond_map = lambda i: ((i * tr) // seq, 0, 0)
    h = pl.pallas_call(
        _norm_mod_kernel,
        grid=(m // tr,),
        in_specs=[
            pl.BlockSpec((tr, d), lambda i: (i, 0)),
            pl.BlockSpec((1, d), lambda i: (0, 0)),
            pl.BlockSpec((1, 1, d), cond_map),
            pl.BlockSpec((1, 1, d), cond_map),
        ],
        out_specs=pl.BlockSpec((tr, d), lambda i: (i, 0)),
        out_shape=jax.ShapeDtypeStruct((m, d), BF16),
        compiler_params=_cparams(1),
        name="norm_modulate",
    )(x2, norm_w.reshape(1, d), scale, shift)
    tm = _pick_tile(m, MM_ROW_TILE)
    tn = _pick_tile(n, COL_TILE)
    return pl.pallas_call(
        _matmul_kernel,
        grid=(m // tm, n // tn),
        in_specs=[
            pl.BlockSpec((tm, d), lambda i, j: (i, 0)),
            pl.BlockSpec((d, tn), lambda i, j: (0, j)),
        ],
        out_specs=pl.BlockSpec((tm, tn), lambda i, j: (i, j)),
        out_shape=jax.ShapeDtypeStruct((m, n), BF16),
        compiler_params=_cparams(2),
        name="in_proj",
    )(h, w)


def _out_kernel(a_ref, b_ref, wa_ref, wb_ref, x_ref, g_ref, o_ref):
    acc = _dot(a_ref[...], wa_ref[...]) + _dot(b_ref[...], wb_ref[...])
    o_ref[...] = x_ref[...] + g_ref[0] * acc


def _out_proj(mix_a, mix_b, w_out, x2, seq, gate):
    m, d = x2.shape
    wdt = mix_a.shape[1]
    nb = gate.shape[0]
    tm = _pick_tile(m, MM_ROW_TILE)
    tn = _pick_tile(d, COL_TILE)
    if nb == 1:
        cond_map = lambda i, j: (0, 0, j)
    else:
        assert seq % tm == 0
        cond_map = lambda i, j: ((i * tm) // seq, 0, j)
    return pl.pallas_call(
        _out_kernel,
        grid=(m // tm, d // tn),
        in_specs=[
            pl.BlockSpec((tm, wdt), lambda i, j: (i, 0)),
            pl.BlockSpec((tm, wdt), lambda i, j: (i, 0)),
            pl.BlockSpec((wdt, tn), lambda i, j: (0, j)),
            pl.BlockSpec((wdt, tn), lambda i, j: (1, j)),
            pl.BlockSpec((tm, tn), lambda i, j: (i, j)),
            pl.BlockSpec((1, 1, tn), cond_map),
        ],
        out_specs=pl.BlockSpec((tm, tn), lambda i, j: (i, j)),
        out_shape=jax.ShapeDtypeStruct((m, d), F32),
        compiler_params=_cparams(2),
        name="out_proj_residual",
    )(mix_a, mix_b, w_out, w_out, x2, gate)


def _attn_ctx0_kernel(q_ref, k_ref, v_ref, g_ref, qw_ref, kw_ref, o_ref, ko_ref, vo_ref):
    hd = LANES
    qscale = hd ** -0.5 * LOG2E
    scores = []
    for hh in range(q_ref.shape[-1] // hd):
        sl = slice(hh * hd, (hh + 1) * hd)
        qn = _rms(q_ref[:, sl].astype(F32), qw_ref[...]) * qscale
        kn = _rms(k_ref[:, sl].astype(F32), kw_ref[...])
        ko_ref[:, sl] = kn
        vo_ref[:, sl] = v_ref[:, sl].astype(F32)
        scores.append(_dot_nt(qn.astype(BF16), kn.astype(BF16)))
    for hh, s in enumerate(scores):
        sl = slice(hh * hd, (hh + 1) * hd)
        p = jnp.exp2(s - jnp.max(s, axis=-1, keepdims=True))
        l = jnp.sum(p, axis=-1, keepdims=True)
        o = _dot(p.astype(BF16), v_ref[:, sl]) / l
        o_ref[:, sl] = (o * _silu(g_ref[:, sl].astype(F32))).astype(o_ref.dtype)


def _attn_ctx0(proj, batch, seq, heads, q_norm, k_norm):
    hd = LANES
    m = batch * seq
    hp = CTX_ATTN_HEADS
    assert heads % hp == 0
    ng = heads // hp
    blk = lambda off: pl.BlockSpec((seq, hp * hd), lambda b, h: (b, off * ng + h))
    vec = pl.BlockSpec((1, hd), lambda b, h: (0, 0))
    return pl.pallas_call(
        _attn_ctx0_kernel,
        grid=(batch, ng),
        in_specs=[blk(0), blk(1), blk(2), blk(3), vec, vec],
        out_specs=[blk(0), blk(0), blk(0)],
        out_shape=[
            jax.ShapeDtypeStruct((m, heads * hd), BF16),
            jax.ShapeDtypeStruct((m, heads * hd), F32),
            jax.ShapeDtypeStruct((m, heads * hd), F32),
        ],
        compiler_params=_cparams(2),
        name="ctx_attention_l0",
    )(proj, proj, proj, proj, q_norm.reshape(1, hd), k_norm.reshape(1, hd))


def _na_bias_plan(rows):
    tiles = rows // NA_ROWS
    assert min(WIN_R, rows) == WIN_R and tiles >= 3
    t_var = np.array([0, 1, tiles - 1])
    ks = np.clip(NA_ROWS * t_var - WIN_R // 2, 0, rows - NA_SPAN)
    r = NA_ROWS * t_var[:, None] + np.arange(NA_ROWS)[None, :]
    rs = np.clip(r - WIN_R // 2, 0, rows - WIN_R)
    kr = ks[:, None] + np.arange(NA_SPAN)[None, :]
    row_ok = (kr[:, None, :] >= rs[:, :, None]) & (kr[:, None, :] < rs[:, :, None] + WIN_R)
    dr = kr[:, None, :] - r[:, :, None] + WIN_R - 1
    return np.where(row_ok, dr, -1)


def _na_bias_kernel(rpb_ref, o_ref, *, plan):
    h = pl.program_id(0)
    n_dr, n_dc = 2 * WIN_R - 1, 2 * WIN_C - 1
    qc = lax.broadcasted_iota(jnp.int32, (GRID_W, GRID_W), 0)
    kc = lax.broadcasted_iota(jnp.int32, (GRID_W, GRID_W), 1)
    cstart = jnp.clip(qc - WIN_C // 2, 0, GRID_W - WIN_C)
    col_ok = (kc >= cstart) & (kc < cstart + WIN_C)
    dc = jnp.clip(kc - qc + WIN_C - 1, 0, n_dc - 1)
    neg = jnp.full((GRID_W, GRID_W), NEG_INF, F32)
    slabs = {-1: neg}
    for dr in sorted(set(int(v) for v in plan.ravel()) - {-1}):
        acc = neg
        for j in range(n_dc):
            acc = jnp.where(col_ok & (dc == j), rpb_ref[(h * n_dr + dr) * n_dc + j] * LOG2E, acc)
        slabs[dr] = acc
    for var in range(plan.shape[0]):
        for i in range(plan.shape[1]):
            row = jnp.concatenate([slabs[int(v)] for v in plan[var, i]], axis=1)
            o_ref[0, var, i * GRID_W:(i + 1) * GRID_W, :] = row


def _na_bias_tables(rpb, rows):
    heads = rpb.shape[0]
    shape = (heads, 3, NA_ROWS * GRID_W, NA_SPAN * GRID_W)
    return pl.pallas_call(
        functools.partial(_na_bias_kernel, plan=_na_bias_plan(rows)),
        grid=(heads,),
        in_specs=[pl.BlockSpec(memory_space=pltpu.SMEM)],
        out_specs=pl.BlockSpec((1,) + shape[1:], lambda h: (h, 0, 0, 0)),
        out_shape=jax.ShapeDtypeStruct(shape, F32),
        compiler_params=_cparams(1),
        name="na_bias_tables",
    )(rpb.astype(F32).reshape(-1))


def _na_kernel(q_ref, k_ref, v_ref, g_ref, kc_ref, vc_ref, b_ref, qw_ref, kw_ref, o_ref, kn_ref, *, rows):
    t = pl.program_id(2)
    hd = LANES
    hp = q_ref.shape[-1] // hd

    @pl.when(t == 0)
    def _():
        for hh in range(hp):
            sl = slice(hh * hd, (hh + 1) * hd)
            kn_ref[:, sl] = _rms(k_ref[:, sl].astype(F32), kw_ref[...]).astype(BF16)

    ks = jnp.clip(NA_ROWS * t - WIN_R // 2, 0, rows - NA_SPAN)
    span = pl.ds(pl.multiple_of(ks * GRID_W, GRID_W), NA_SPAN * GRID_W)
    qscale = hd ** -0.5 * LOG2E
    scores = []
    for hh in range(hp):
        sl = slice(hh * hd, (hh + 1) * hd)
        qn = (_rms(q_ref[:, sl].astype(F32), qw_ref[...]) * qscale).astype(BF16)
        scores.append((_dot_nt(qn, kn_ref[span, sl]) + b_ref[hh, 0], _dot_nt(qn, kc_ref[0, :, sl])))
    for hh, (s_lat, s_ctx) in enumerate(scores):
        sl = slice(hh * hd, (hh + 1) * hd)
        mx = jnp.maximum(jnp.max(s_lat, axis=-1, keepdims=True), jnp.max(s_ctx, axis=-1, keepdims=True))
        p_lat = jnp.exp2(s_lat - mx)
        p_ctx = jnp.exp2(s_ctx - mx)
        l = jnp.sum(p_lat, axis=-1, keepdims=True) + jnp.sum(p_ctx, axis=-1, keepdims=True)
        o = (_dot(p_lat.astype(BF16), v_ref[span, sl]) + _dot(p_ctx.astype(BF16), vc_ref[0, :, sl])) / l
        o_ref[:, sl] = (o * _silu(g_ref[:, sl].astype(F32))).astype(o_ref.dtype)


def _na_latent(proj, batch, seq, heads, k_ctx, v_ctx, bias, q_norm, k_norm):
    hd = LANES
    rows = seq // GRID_W
    tiles = rows // NA_ROWS
    tq = NA_ROWS * GRID_W
    lc = k_ctx.shape[1]
    hp = LAT_ATTN_HEADS
    assert heads % hp == 0
    ng = heads // hp
    qblk = lambda off: pl.BlockSpec((tq, hp * hd), lambda b, h, t: (b * tiles + t, off * ng + h))
    full = lambda off: pl.BlockSpec((seq, hp * hd), lambda b, h, t: (b, off * ng + h))
    ctx = pl.BlockSpec((1, lc, hp * hd), lambda b, h, t: (b, 0, h))
    variant = lambda t: jnp.where(t == 0, 0, jnp.where(t == tiles - 1, 2, 1))
    vec = pl.BlockSpec((1, hd), lambda b, h, t: (0, 0))
    return pl.pallas_call(
        functools.partial(_na_kernel, rows=rows),
        grid=(batch, ng, tiles),
        in_specs=[
            qblk(0), full(1), full(2), qblk(3), ctx, ctx,
            pl.BlockSpec((hp, 1, tq, NA_SPAN * GRID_W), lambda b, h, t: (h, variant(t), 0, 0)),
            vec, vec,
        ],
        out_specs=qblk(0),
        out_shape=jax.ShapeDtypeStruct((batch * seq, heads * hd), BF16),
        scratch_shapes=[pltpu.VMEM((seq, hp * hd), BF16)],
        compiler_params=_cparams(3),
        name="neighbourhood_attention",
    )(proj, proj, proj, proj, k_ctx, v_ctx, bias, q_norm.reshape(1, hd), k_norm.reshape(1, hd))


def _sg_kernel(u_ref, v_ref, g_ref, nw_ref, w_ref, b_ref, o_ref):
    groups = w_ref.shape[0]
    hd = w_ref.shape[-1]
    u = _gelu(u_ref[...].astype(F32))
    v = _rms(_gelu(v_ref[...].astype(F32)), nw_ref[...]).astype(BF16)
    g = _silu(g_ref[...].astype(F32))
    for gi in range(groups):
        sl = slice(gi * hd, (gi + 1) * hd)
        mixed = _dot(w_ref[gi], v[:, sl]) + b_ref[gi]
        o_ref[:, sl] = (u[:, sl] * mixed * g[:, sl]).astype(o_ref.dtype)


def _spatial_gating(proj, width, sg_norm, sg_w, sg_b):
    m = proj.shape[0]
    groups = sg_w.shape[0]
    blk = lambda off: pl.BlockSpec((SG_CHUNK, width), lambda i: (i, off))
    bias = jnp.broadcast_to(sg_b.astype(F32)[:, :, None], (groups, SG_CHUNK, LANES))
    return pl.pallas_call(
        _sg_kernel,
        grid=(m // SG_CHUNK,),
        in_specs=[
            blk(4), blk(5), blk(6),
            pl.BlockSpec((1, width), lambda i: (0, 0)),
            pl.BlockSpec((groups, SG_CHUNK, SG_CHUNK), lambda i: (0, 0, 0)),
            pl.BlockSpec((groups, SG_CHUNK, LANES), lambda i: (0, 0, 0)),
        ],
        out_specs=pl.BlockSpec((SG_CHUNK, width), lambda i: (i, 0)),
        out_shape=jax.ShapeDtypeStruct((m, width), BF16),
        compiler_params=_cparams(1),
        name="spatial_gating",
    )(proj, proj, proj, sg_norm.reshape(1, width), sg_w.astype(BF16), bias)


def _scan_matrix(rev):
    t = np.arange(SCAN_BLOCK)
    same = (t[:, None] // SCAN_CHUNK) == (t[None, :] // SCAN_CHUNK)
    incl = same & ((t[None, :] >= t[:, None]) if rev else (t[None, :] <= t[:, None]))
    return jnp.asarray(incl, dtype=BF16)


def _chunk_rows(a, pos):
    parts = []
    for c in range(a.shape[0] // SCAN_CHUNK):
        row = a[c * SCAN_CHUNK + pos:c * SCAN_CHUNK + pos + 1]
        parts.append(jnp.broadcast_to(row, (SCAN_CHUNK, a.shape[1])))
    return jnp.concatenate(parts, axis=0)


def _hgrn_block(q, z, v, lb, tri, states, rev):
    hd = LANES
    nc = q.shape[0] // SCAN_CHUNK
    half = SCAN_CHUNK // 2
    f = lb + (1.0 - lb) * jax.nn.sigmoid(z)
    g = jnp.log(f)
    k = 1.0 - f
    g_hi = g.astype(BF16)
    g_lo = (g - g_hi.astype(F32)).astype(BF16)
    a = _dot(tri, g_hi) + _dot(tri, g_lo)
    tot_row = 0 if rev else SCAN_CHUNK - 1
    a_ref = _chunk_rows(a, half if rev else half - 1)
    a_tot = _chunk_rows(a, tot_row)
    qe = (q * jnp.exp(jnp.minimum(a - a_ref, EXP_CLAMP))).astype(BF16)
    ke = (k * jnp.exp(jnp.minimum(a_ref - a, EXP_CLAMP))).astype(BF16)
    qa = (q * jnp.exp(a)).astype(BF16)
    kd = (k * jnp.exp(a_tot - a)).astype(BF16)
    heads = [slice(h * hd, (h + 1) * hd) for h in range(len(states))]
    chunks = [slice(c * SCAN_CHUNK, (c + 1) * SCAN_CHUNK) for c in range(nc)]
    dec = [jnp.exp(a[c * SCAN_CHUNK + tot_row:c * SCAN_CHUNK + tot_row + 1]) for c in range(nc)]
    probs = [_dot_nt(qe[:, sl], ke[:, sl]) for sl in heads]
    upd = [[_dot_tn(v[cs, sl], kd[cs, sl]) for cs in chunks] for sl in heads]
    in_chunk = tri > 0
    intra = [_dot(jnp.where(in_chunk, p.astype(BF16), 0.0), v[:, sl]) for p, sl in zip(probs, heads)]
    outs, new_states = [], []
    for h, sl in enumerate(heads):
        state = states[h]
        starts = [None] * nc
        for c in (range(nc - 1, -1, -1) if rev else range(nc)):
            starts[c] = state.astype(BF16)
            state = state * dec[c][:, sl] + upd[h][c]
        inter = [_dot_nt(qa[cs, sl], starts[c]) for c, cs in enumerate(chunks)]
        outs.append(intra[h] + jnp.concatenate(inter, axis=0))
        new_states.append(state)
    return outs, new_states


def _hgrn_kernel(*refs, rev, has_s0, emit_state, final):
    refs = list(refs)
    q_ref, z_ref, i_ref, lb_ref, tri_ref = refs[:5]
    del refs[:5]
    s0_ref = refs.pop(0) if has_s0 else None
    if final:
        of_ref, gate_ref, onw_ref = refs[:3]
        del refs[:3]
    o_ref = refs.pop(0)
    st_ref = refs.pop(0) if emit_state else None
    state_ref = refs.pop(0)
    t = pl.program_id(2)
    hd = LANES
    tri = tri_ref[...]

    hp = state_ref.shape[0]

    @pl.when(t == 0)
    def _():
        for hh in range(hp):
            state_ref[hh] = s0_ref[0, hh].T if has_s0 else jnp.zeros((hd, hd), F32)

    outs, states = _hgrn_block(q_ref[...].astype(F32), z_ref[...].astype(F32), i_ref[...], lb_ref[...], tri,
                               [state_ref[hh] for hh in range(hp)], rev)
    for hh in range(hp):
        sl = slice(hh * hd, (hh + 1) * hd)
        state_ref[hh] = states[hh]
        if final:
            o = _rms(outs[hh] + of_ref[:, sl], onw_ref[...])
            o_ref[:, sl] = (o * _silu(gate_ref[:, sl].astype(F32))).astype(o_ref.dtype)
        else:
            o_ref[:, sl] = outs[hh]

    if emit_state:
        @pl.when(t == pl.num_programs(2) - 1)
        def _():
            for hh in range(hp):
                st_ref[0, hh] = state_ref[hh].T


def _hgrn_pass(proj, batch, seq, heads, lb, s0, *, rev, emit_state, prev=None, out_norm=None):
    hd = LANES
    tb = _pick_tile(seq, SCAN_BLOCK)
    assert tb == SCAN_BLOCK
    nblk = seq // tb
    hp = min(SCAN_HEADS, heads)
    assert heads % hp == 0
    ng = heads // hp
    final = prev is not None
    tmap = (lambda t: nblk - 1 - t) if rev else (lambda t: t)
    blk = lambda off: pl.BlockSpec((tb, hp * hd), lambda b, h, t: (b * nblk + tmap(t), off * ng + h))
    in_specs = [
        blk(0), blk(2 if rev else 1), blk(3),
        pl.BlockSpec((1, hp * hd), lambda b, h, t: (0, h)),
        pl.BlockSpec((tb, tb), lambda b, h, t: (0, 0)),
    ]
    args = [proj, proj, proj, lb.reshape(1, heads * hd), _scan_matrix(rev)]
    if s0 is not None:
        in_specs.append(pl.BlockSpec((1, hp, hd, hd), lambda b, h, t: (b, h, 0, 0)))
        args.append(s0)
    if final:
        in_specs += [blk(0), blk(4), pl.BlockSpec((1, hd), lambda b, h, t: (0, 0))]
        args += [prev, proj, out_norm.reshape(1, hd)]
    out_specs = [blk(0)]
    out_shape = [jax.ShapeDtypeStruct((batch * seq, heads * hd), BF16 if final else F32)]
    if emit_state:
        out_specs.append(pl.BlockSpec((1, hp, hd, hd), lambda b, h, t: (b, h, 0, 0)))
        out_shape.append(jax.ShapeDtypeStruct((batch, heads, hd, hd), F32))
    res = pl.pallas_call(
        functools.partial(_hgrn_kernel, rev=rev, has_s0=s0 is not None, emit_state=emit_state, final=final),
        grid=(batch, ng, nblk),
        in_specs=in_specs,
        out_specs=out_specs,
        out_shape=out_shape,
        scratch_shapes=[pltpu.VMEM((hp, hd, hd), F32)],
        compiler_params=_cparams(3),
        name="hgrn_scan_bwd" if rev else "hgrn_scan_fwd",
    )(*args)
    return res if emit_state else (res[0], None)


_ROPE_SWAP = np.concatenate([np.arange(16, 32), np.arange(0, 16), np.arange(48, 64), np.arange(32, 48)])


def _rope_table(seq, rotate):
    if not rotate:
        row = np.concatenate([np.ones(ROPE_D), np.zeros(ROPE_D)]).astype(np.float32)
        return jnp.asarray(np.broadcast_to(row, (seq, 2 * ROPE_D)))
    t = jnp.arange(seq)
    half = ROPE_D // 4
    inv = ROPE_BASE ** (-jnp.arange(half, dtype=F32) / half)
    ang_r = (t // GRID_W).astype(F32)[:, None] * inv[None, :]
    ang_c = (t % GRID_W).astype(F32)[:, None] * inv[None, :]
    cos = jnp.concatenate([jnp.cos(ang_r)] * 2 + [jnp.cos(ang_c)] * 2, axis=-1)
    sin = jnp.concatenate([-jnp.sin(ang_r), jnp.sin(ang_r), -jnp.sin(ang_c), jnp.sin(ang_c)], axis=-1)
    return jnp.concatenate([cos, sin], axis=-1)


def _rotate_rope(r, tab, lane_lo):
    rot = r * tab
    rot = rot + pltpu.roll(rot, ROPE_D, 1)
    return jnp.where(lane_lo, rot, 0.0)


def _qup_kernel(cq_ref, nw_ref, w_ref, gn_ref, gr_ref, tab_ref, o_ref, h_ref):
    @pl.when(pl.program_id(1) == 0)
    def _():
        h_ref[...] = _rms(cq_ref[...].astype(F32), nw_ref[...]).astype(BF16)

    hw = NOPE_D + 2 * ROPE_D
    scale = (NOPE_D + ROPE_D) ** -0.5 * LOG2E
    lane_lo = lax.broadcasted_iota(jnp.int32, (1, LANES), 1) < ROPE_D
    tab = tab_ref[...] * gr_ref[...]
    y = _dot(h_ref[...], w_ref[...])
    ones_all = _ones_rows(LANES)
    ones_lo = _ones_rows(ROPE_D)
    gain_n = gn_ref[...] * scale
    for hh in range(y.shape[1] // hw):
        yn = y[:, hh * hw:hh * hw + NOPE_D]
        yr = y[:, hh * hw + NOPE_D:(hh + 1) * hw]
        o_ref[:, hh * hw:hh * hw + NOPE_D] = _head_rms(yn, gain_n, ones_all).astype(o_ref.dtype)
        ms_r = _dot((yr * yr).astype(BF16), ones_lo) * (1.0 / ROPE_D)
        qr = _rotate_rope(yr, tab, lane_lo) * (lax.rsqrt(ms_r + EPS) * scale)
        o_ref[:, hh * hw + NOPE_D:(hh + 1) * hw] = qr.astype(o_ref.dtype)


def _q_up(proj, col_blk, q_lora, seq, a_norm, w, gain_n, gain_r, table):
    m = proj.shape[0]
    n = w.shape[1]
    tm = _pick_tile(seq, ROW_TILE)
    tn = _pick_tile(n, COL_TILE)
    nt = table.shape[0] // tm
    return pl.pallas_call(
        _qup_kernel,
        grid=(m // tm, n // tn),
        in_specs=[
            pl.BlockSpec((tm, q_lora), lambda i, j: (i, col_blk)),
            pl.BlockSpec((1, q_lora), lambda i, j: (0, 0)),
            pl.BlockSpec((q_lora, tn), lambda i, j: (0, j)),
            pl.BlockSpec((1, NOPE_D), lambda i, j: (0, 0)),
            pl.BlockSpec((1, 2 * ROPE_D), lambda i, j: (0, 0)),
            pl.BlockSpec((tm, 2 * ROPE_D), lambda i, j: (i % nt, 0)),
        ],
        out_specs=pl.BlockSpec((tm, tn), lambda i, j: (i, j)),
        out_shape=jax.ShapeDtypeStruct((m, n), BF16),
        scratch_shapes=[pltpu.VMEM((tm, q_lora), BF16)],
        compiler_params=_cparams(2),
        name="mla_q_up",
    )(proj, a_norm.reshape(1, q_lora), w, gain_n.reshape(1, NOPE_D), gain_r.reshape(1, 2 * ROPE_D), table)


def _kvup_kernel(*refs, normalize_in, emit_cache, heads):
    refs = list(refs)
    ckv_ref, kr_ref, nw_ref, gkr_ref, w_ref, gk_ref, tab_ref, k_ref, v_ref = refs[:9]
    lane_lo = lax.broadcasted_iota(jnp.int32, (1, LANES), 1) < ROPE_D
    c = ckv_ref[...].astype(F32)
    r = kr_ref[...].astype(F32)
    if normalize_in:
        c = _rms(c, nw_ref[...])
        ms = jnp.sum(jnp.where(lane_lo, r * r, 0.0), axis=-1, keepdims=True) * (1.0 / ROPE_D)
        r = r * lax.rsqrt(ms + EPS) * gkr_ref[...]
    if emit_cache:
        refs[9][...] = c
        refs[10][...] = r[:, :ROPE_D]
    k_rope = _rotate_rope(r, tab_ref[...], lane_lo).astype(k_ref.dtype)
    y = _dot(c.astype(BF16), w_ref[...])
    hw = 2 * LANES
    for h in range(heads):
        k_ref[:, h * hw:h * hw + NOPE_D] = _rms(y[:, h * hw:h * hw + NOPE_D], gk_ref[...]).astype(k_ref.dtype)
        k_ref[:, h * hw + NOPE_D:(h + 1) * hw] = k_rope
        v_ref[:, h * LANES:(h + 1) * LANES] = y[:, h * hw + NOPE_D:(h + 1) * hw].astype(v_ref.dtype)


def _kv_up(ckv_src, ckv_blk, kr_src, kr_blk, seq, kv_lora, heads, a_norm, gain_kr, w, gain_k, table,
           *, normalize_in, emit_cache):
    m = ckv_src.shape[0]
    tm = _pick_tile(seq, ROW_TILE)
    nt = table.shape[0] // tm
    out_specs = [
        pl.BlockSpec((tm, heads * 2 * LANES), lambda i: (i, 0)),
        pl.BlockSpec((tm, heads * LANES), lambda i: (i, 0)),
    ]
    out_shape = [
        jax.ShapeDtypeStruct((m, heads * 2 * LANES), BF16),
        jax.ShapeDtypeStruct((m, heads * LANES), BF16),
    ]
    if emit_cache:
        out_specs += [pl.BlockSpec((tm, kv_lora), lambda i: (i, 0)), pl.BlockSpec((tm, ROPE_D), lambda i: (i, 0))]
        out_shape += [jax.ShapeDtypeStruct((m, kv_lora), F32), jax.ShapeDtypeStruct((m, ROPE_D), F32)]
    return pl.pallas_call(
        functools.partial(_kvup_kernel, normalize_in=normalize_in, emit_cache=emit_cache, heads=heads),
        grid=(m // tm,),
        in_specs=[
            pl.BlockSpec((tm, kv_lora), lambda i: (i, ckv_blk)),
            pl.BlockSpec((tm, 2 * ROPE_D), lambda i: (i, kr_blk)),
            pl.BlockSpec((1, kv_lora), lambda i: (0, 0)),
            pl.BlockSpec((1, 2 * ROPE_D), lambda i: (0, 0)),
            pl.BlockSpec((kv_lora, heads * 2 * LANES), lambda i: (0, 0)),
            pl.BlockSpec((1, NOPE_D), lambda i: (0, 0)),
            pl.BlockSpec((tm, 2 * ROPE_D), lambda i: (i % nt, 0)),
        ],
        out_specs=out_specs,
        out_shape=out_shape,
        compiler_params=_cparams(1),
        name="mla_kv_up",
    )(ckv_src, kr_src, a_norm.reshape(1, kv_lora), gain_kr.reshape(1, 2 * ROPE_D), w,
      gain_k.reshape(1, NOPE_D), table)


def _mla_attn_kernel(*refs, has_ctx):
    if has_ctx:
        q_ref, k_ref, v_ref, kc_ref, vc_ref, g_ref, o_ref, vt_ref, vct_ref = refs
    else:
        q_ref, k_ref, v_ref, g_ref, o_ref, vt_ref = refs
    qw = 2 * LANES
    hp = q_ref.shape[-1] // qw

    @pl.when(pl.program_id(2) == 0)
    def _():
        for hh in range(hp):
            sl = slice(hh * LANES, (hh + 1) * LANES)
            vt_ref[hh] = v_ref[:, sl].astype(F32).T.astype(BF16)
            if has_ctx:
                vct_ref[hh] = vc_ref[:, sl].astype(F32).T.astype(BF16)

    scores = []
    for hh in range(hp):
        q = q_ref[:, hh * qw:(hh + 1) * qw]
        s = _dot_nt(k_ref[:, hh * qw:(hh + 1) * qw], q)
        s_c = _dot_nt(kc_ref[:, hh * qw:(hh + 1) * qw], q) if has_ctx else None
        scores.append((s, s_c))
    for hh, (s, s_c) in enumerate(scores):
        sl = slice(hh * LANES, (hh + 1) * LANES)
        mx = jnp.max(s, axis=0, keepdims=True)
        if has_ctx:
            mx = jnp.maximum(mx, jnp.max(s_c, axis=0, keepdims=True))
        p = jnp.exp2(s - mx)
        l = jnp.sum(p, axis=0, keepdims=True)
        acc = _dot(vt_ref[hh], p.astype(BF16))
        if has_ctx:
            p_c = jnp.exp2(s_c - mx)
            l = l + jnp.sum(p_c, axis=0, keepdims=True)
            acc = acc + _dot(vct_ref[hh], p_c.astype(BF16))
        o = (acc / l).T
        o_ref[:, sl] = (o * _silu(g_ref[:, sl].astype(F32))).astype(o_ref.dtype)


def _mla_attention(q, k, v, proj, gate_blk, batch, seq, heads, hp, ctx=None):
    tq = _pick_tile(seq, ATTN_Q)
    nq = seq // tq
    assert heads % hp == 0 and gate_blk % hp == 0
    qmap = lambda b, h, t: (b * nq + t, h)
    in_specs = [
        pl.BlockSpec((tq, hp * 2 * LANES), qmap),
        pl.BlockSpec((seq, hp * 2 * LANES), lambda b, h, t: (b, h)),
        pl.BlockSpec((seq, hp * LANES), lambda b, h, t: (b, h)),
    ]
    args = [q, k, v]
    if ctx is not None:
        k_c, v_c = ctx
        lc = k_c.shape[0] // batch
        in_specs += [
            pl.BlockSpec((lc, hp * 2 * LANES), lambda b, h, t: (b, h)),
            pl.BlockSpec((lc, hp * LANES), lambda b, h, t: (b, h)),
        ]
        args += [k_c, v_c]
    in_specs.append(pl.BlockSpec((tq, hp * LANES), lambda b, h, t: (b * nq + t, gate_blk // hp + h)))
    args.append(proj)
    scratch = [pltpu.VMEM((hp, LANES, seq), BF16)]
    if ctx is not None:
        scratch.append(pltpu.VMEM((hp, LANES, lc), BF16))
    return pl.pallas_call(
        functools.partial(_mla_attn_kernel, has_ctx=ctx is not None),
        grid=(batch, heads // hp, nq),
        in_specs=in_specs,
        out_specs=pl.BlockSpec((tq, hp * LANES), qmap),
        out_shape=jax.ShapeDtypeStruct((batch * seq, heads * LANES), BF16),
        scratch_shapes=scratch,
        compiler_params=_cparams(3),
        name="mla_attention",
    )(*args)


def _relayout_kernel(w_ref, o_ref, *, plan):
    end = 0
    for dst, src, wd in plan:
        o_ref[:, dst:dst + wd] = w_ref[:, src:src + wd].astype(o_ref.dtype)
        end = max(end, dst + wd)
    if end < o_ref.shape[1]:
        o_ref[:, end:] = jnp.zeros((o_ref.shape[0], o_ref.shape[1] - end), o_ref.dtype)


def _relayout_columns(w, plan, n_out):
    k, n = w.shape
    tr = _pick_tile(k, RELAYOUT_ROWS)
    return pl.pallas_call(
        functools.partial(_relayout_kernel, plan=plan),
        grid=(k // tr,),
        in_specs=[pl.BlockSpec((tr, n), lambda i: (i, 0))],
        out_specs=pl.BlockSpec((tr, n_out), lambda i: (i, 0)),
        out_shape=jax.ShapeDtypeStruct((k, n_out), BF16),
        compiler_params=_cparams(1),
        name="weight_relayout",
    )(w)


def _pad_cols(w, mult):
    pad = (-w.shape[1]) % mult
    return w if pad == 0 else jnp.concatenate([w, jnp.zeros((w.shape[0], pad), w.dtype)], axis=1)


def kernel(x_prompt, x_sample, cache_na_k, cache_na_v, state_hgrn, cache_mla_ckv, cache_mla_krope, c, c_ctx,
           norm_w, w_ada, b_ada, w_out,
           w_in_ab, na_q_norm, na_k_norm, na_rpb, sg_norm, sg_w, sg_b,
           w_in_cd, hgrn_lb, hgrn_out_norm, mla_q_a_norm, mla_w_q_up, mla_kv_a_norm, mla_w_kv_up,
           mla_q_norm, mla_k_norm):
    bp, lp, d = x_prompt.shape
    bs, ls, _ = x_sample.shape
    depth = norm_w.shape[0]
    hd = cache_na_k.shape[-1]
    heads = cache_na_k.shape[-2]
    width = heads * hd
    assert hd == LANES and state_hgrn.shape[-2:] == (hd, hd) and ls % GRID_W == 0
    q_lora = mla_q_a_norm.shape[-1]
    kv_lora = mla_kv_a_norm.shape[-1]
    assert mla_q_norm.shape[-1] == NOPE_D + ROPE_D and cache_mla_krope.shape[-1] == ROPE_D
    assert mla_w_kv_up.shape[-1] == heads * 2 * LANES

    n_cond = bs + 1
    cond_rows = -(-n_cond // 8) * 8
    cond = jnp.concatenate([c, c_ctx[None, :], jnp.zeros((cond_rows - n_cond, d), F32)], axis=0)
    mod = _modulation(cond, w_ada, b_ada).reshape(depth, cond_rows, 3, 1, d)

    lb_soft = jax.nn.softmax(hgrn_lb.astype(F32), axis=0)
    lb_cum = jnp.cumsum(lb_soft, axis=0)
    lower_bounds = lb_cum - lb_cum[:1]

    xp = x_prompt.reshape(bp * lp, d)
    xs = x_sample.reshape(bs * ls, d)
    na_k_new, na_v_new, hgrn_new, ckv_new, kr_new = [], [], [], [], []
    for layer in range(depth):
        j = layer // 2
        shift_s, scale_s, gate_s = (mod[layer, :bs, i] for i in range(3))
        shift_p, scale_p, gate_p = (mod[layer, bs:bs + 1, i] for i in range(3))
        w_o = w_out[layer].astype(BF16)
        if layer % 2 == 0:
            w_in = _pad_cols(w_in_ab[j].astype(BF16), COL_TILE)
            proj_p = _in_proj(xp, lp, norm_w[layer], scale_p, shift_p, w_in)
            proj_s = _in_proj(xs, ls, norm_w[layer], scale_s, shift_s, w_in)
            mix_a_p, k_new, v_new = _attn_ctx0(proj_p, bp, lp, heads, na_q_norm[j], na_k_norm[j])
            na_k_new.append(k_new.reshape(bp, lp, heads, hd))
            na_v_new.append(v_new.reshape(bp, lp, heads, hd))
            bias = _na_bias_tables(na_rpb[j], ls // GRID_W)
            lc = cache_na_k.shape[2]
            mix_a_s = _na_latent(proj_s, bs, ls, heads,
                                 cache_na_k[:, j].reshape(bs, lc, width).astype(BF16),
                                 cache_na_v[:, j].reshape(bs, lc, width).astype(BF16),
                                 bias, na_q_norm[j], na_k_norm[j])
            mix_b_p = _spatial_gating(proj_p, width, sg_norm[j], sg_w[j], sg_b[j])
            mix_b_s = _spatial_gating(proj_s, width, sg_norm[j], sg_w[j], sg_b[j])
        else:
            kr_src = 5 * width + q_lora + kv_lora
            gd_src = kr_src + ROPE_D
            kr_dst = 6 * width + q_lora + kv_lora
            half = ROPE_D // 4
            plan = [(0, 0, 5 * width), (5 * width, gd_src, width), (6 * width, 5 * width, q_lora + kv_lora),
                    (kr_dst, kr_src, ROPE_D)]
            plan += [(kr_dst + ROPE_D + i * half, kr_src + int(_ROPE_SWAP[i * half]), half) for i in range(4)]
            w_in = _relayout_columns(w_in_cd[j], plan, -(-(kr_dst + 2 * ROPE_D) // COL_TILE) * COL_TILE)
            assert (6 * width) % q_lora == 0 and (6 * width + q_lora) % kv_lora == 0
            cq_blk = (6 * width) // q_lora
            ckv_blk = (6 * width + q_lora) // kv_lora
            kr_blk = (6 * width + q_lora + kv_lora) // LANES
            gd_blk = (5 * width) // LANES
            proj_p = _in_proj(xp, lp, norm_w[layer], scale_p, shift_p, w_in)
            proj_s = _in_proj(xs, ls, norm_w[layer], scale_s, shift_s, w_in)

            lb_f, lb_b = lower_bounds[layer, 0], lower_bounds[layer, 1]
            of_p, st_f = _hgrn_pass(proj_p, bp, lp, heads, lb_f, None, rev=False, emit_state=True)
            mix_a_p, st_b = _hgrn_pass(proj_p, bp, lp, heads, lb_b, None, rev=True, emit_state=True,
                                       prev=of_p, out_norm=hgrn_out_norm[j])
            hgrn_new.append(jnp.stack([st_f, st_b], axis=1))
            of_s, _ = _hgrn_pass(proj_s, bs, ls, heads, lb_f, state_hgrn[:, j, 0], rev=False, emit_state=False)
            mix_a_s, _ = _hgrn_pass(proj_s, bs, ls, heads, lb_b, state_hgrn[:, j, 1], rev=True, emit_state=False,
                                    prev=of_s, out_norm=hgrn_out_norm[j])

            wq = mla_w_q_up[j].reshape(q_lora, heads, NOPE_D + ROPE_D)
            wq = jnp.concatenate([wq, wq[:, :, NOPE_D:][:, :, _ROPE_SWAP]], axis=-1)
            wq = wq.reshape(q_lora, heads * 2 * LANES).astype(BF16)
            wkv = mla_w_kv_up[j].astype(BF16)
            gq, gk = mla_q_norm[j], mla_k_norm[j]
            gq_r = jnp.concatenate([gq[NOPE_D:], gq[NOPE_D:][_ROPE_SWAP]])
            gk_r = jnp.concatenate([gk[NOPE_D:], gk[NOPE_D:][_ROPE_SWAP]])
            tab_p = _rope_table(_pick_tile(lp, ROW_TILE), False)
            tab_s = _rope_table(ls, True)
            tab_c = _rope_table(_pick_tile(cache_mla_ckv.shape[2], ROW_TILE), False)

            q_p = _q_up(proj_p, cq_blk, q_lora, lp, mla_q_a_norm[j], wq, gq[:NOPE_D], gq_r, tab_p)
            k_p, v_p, ckv_n, kr_n = _kv_up(proj_p, ckv_blk, proj_p, kr_blk, lp, kv_lora, heads, mla_kv_a_norm[j],
                                           gk_r, wkv, gk[:NOPE_D], tab_p, normalize_in=True, emit_cache=True)
            ckv_new.append(ckv_n.reshape(bp, lp, kv_lora))
            kr_new.append(kr_n.reshape(bp, lp, ROPE_D))
            mix_b_p = _mla_attention(q_p, k_p, v_p, proj_p, gd_blk, bp, lp, heads, CTX_ATTN_HEADS)

            q_s = _q_up(proj_s, cq_blk, q_lora, ls, mla_q_a_norm[j], wq, gq[:NOPE_D], gq_r, tab_s)
            k_s, v_s = _kv_up(proj_s, ckv_blk, proj_s, kr_blk, ls, kv_lora, heads, mla_kv_a_norm[j],
                              gk_r, wkv, gk[:NOPE_D], tab_s, normalize_in=True, emit_cache=False)
            lc = cache_mla_ckv.shape[2]
            kr_c = cache_mla_krope[:, j].reshape(bs * lc, ROPE_D)
            kr_c = jnp.concatenate([kr_c, jnp.zeros_like(kr_c)], axis=1)
            k_c, v_c = _kv_up(cache_mla_ckv[:, j].reshape(bs * lc, kv_lora), 0, kr_c, 0, lc, kv_lora, heads,
                              mla_kv_a_norm[j], gk_r, wkv, gk[:NOPE_D], tab_c, normalize_in=False, emit_cache=False)
            mix_b_s = _mla_attention(q_s, k_s, v_s, proj_s, gd_blk, bs, ls, heads, LAT_ATTN_HEADS, ctx=(k_c, v_c))
        xp = _out_proj(mix_a_p, mix_b_p, w_o, xp, lp, gate_p)
        xs = _out_proj(mix_a_s, mix_b_s, w_o, xs, ls, gate_s)
    return (xp.reshape(bp, lp, d), xs.reshape(bs, ls, d),
            jnp.stack(na_k_new, axis=1), jnp.stack(na_v_new, axis=1), jnp.stack(hgrn_new, axis=1),
            jnp.stack(ckv_new, axis=1), jnp.stack(kr_new, axis=1))
```

```python
import functools

import numpy as np
import jax
import jax.numpy as jnp
from jax import lax
from jax.experimental import pallas as pl
from jax.experimental.pallas import tpu as pltpu

F32 = jnp.float32
BF16 = jnp.bfloat16

GRID_W = 64
WIN_R = 8
WIN_C = 16
SG_CHUNK = 128
NOPE_D = 128
ROPE_D = 64
ROPE_BASE = 10000.0
EPS = 1e-6
NEG_INF = -1e30

LANES = 128
VMEM_LIMIT = 56 * 1024 * 1024

ROW_TILE = 512
MM_ROW_TILE = 1024
COL_TILE = 1024
NA_ROWS = 4
NA_SPAN = NA_ROWS + WIN_R
ATTN_Q = 512
SCAN_BLOCK = 256
SCAN_CHUNK = 32
SCAN_HEADS = 16
CTX_ATTN_HEADS = 8
LAT_ATTN_HEADS = 2
NA_HEADS = 4
SG_CHUNKS_PER_STEP = 2
LOG2E = 1.4426950408889634
RELAYOUT_ROWS = 256
EXP_CLAMP = 80.0


def _cparams(n_axes, **extra):
    return pltpu.CompilerParams(
        dimension_semantics=("arbitrary",) * n_axes, vmem_limit_bytes=VMEM_LIMIT, **extra)


def _pick_tile(n, pref):
    t = pref
    while n % t:
        t //= 2
    assert t >= LANES or t == n, (n, pref)
    return t


def _rms(x, w):
    return x * lax.rsqrt(jnp.mean(x * x, axis=-1, keepdims=True) + EPS) * w


def _ones_rows(n):
    row = lax.broadcasted_iota(jnp.int32, (LANES, LANES), 0)
    return jnp.where(row < n, 1.0, 0.0).astype(BF16)


def _head_rms(x, w, ones, n=LANES):
    ms = _dot((x * x).astype(BF16), ones) * (1.0 / n)
    return x * lax.rsqrt(ms + EPS) * w


def _silu(x):
    return x * jax.nn.sigmoid(x)


def _gelu(x):
    return 0.5 * x * (1.0 + jnp.tanh(0.7978845608028654 * (x + 0.044715 * (x * x * x))))


def _dot(a, b):
    return jnp.dot(a, b, preferred_element_type=F32)


def _dot_nt(a, b):
    return lax.dot_general(a, b, (((1,), (1,)), ((), ())), preferred_element_type=F32)


def _dot_tn(a, b):
    return lax.dot_general(a, b, (((0,), (0,)), ((), ())), preferred_element_type=F32)


def _mod_kernel(c_ref, w_ref, b_ref, o_ref):
    a = _silu(c_ref[...]).astype(BF16)
    o_ref[0] = _dot(a, w_ref[0].astype(BF16)) + b_ref[0]


def _modulation(cond, w_ada, b_ada):
    depth, d, n = w_ada.shape
    r = cond.shape[0]
    tn = _pick_tile(n, 512)
    return pl.pallas_call(
        _mod_kernel,
        grid=(depth, n // tn),
        in_specs=[
            pl.BlockSpec((r, d), lambda l, j: (0, 0)),
            pl.BlockSpec((1, d, tn), lambda l, j: (l, 0, j)),
            pl.BlockSpec((1, 1, tn), lambda l, j: (l, 0, j)),
        ],
        out_specs=pl.BlockSpec((1, r, tn), lambda l, j: (l, 0, j)),
        out_shape=jax.ShapeDtypeStruct((depth, r, n), F32),
        compiler_params=_cparams(2),
        name="adaln_modulation",
    )(cond, w_ada, b_ada.reshape(depth, 1, n))


def _norm_mod_kernel(x_ref, nw_ref, sc_ref, sh_ref, o_ref):
    h = _rms(x_ref[...], nw_ref[...])
    o_ref[...] = (h * (1.0 + sc_ref[0]) + sh_ref[0]).astype(o_ref.dtype)


def _matmul_kernel(a_ref, w_ref, o_ref):
    o_ref[...] = _dot(a_ref[...], w_ref[...]).astype(o_ref.dtype)


def _in_proj(x2, seq, norm_w, scale, shift, w):
    m, d = x2.shape
    n = w.shape[1]
    nb = scale.shape[0]
    tr = _pick_tile(m, ROW_TILE)
    if nb == 1:
        cond_map = lambda i: (0, 0, 0)
    else:
        assert seq % tr == 0
        cond_map = lambda i: ((i * tr) // seq, 0, 0)
    h = pl.pallas_call(
        _norm_mod_kernel,
        grid=(m // tr,),
        in_specs=[
            pl.BlockSpec((tr, d), lambda i: (i, 0)),
            pl.BlockSpec((1, d), lambda i: (0, 0)),
            pl.BlockSpec((1, 1, d), cond_map),
            pl.BlockSpec((1, 1, d), cond_map),
        ],
        out_specs=pl.BlockSpec((tr, d), lambda i: (i, 0)),
        out_shape=jax.ShapeDtypeStruct((m, d), BF16),
        compiler_params=_cparams(1),
        name="norm_modulate",
    )(x2, norm_w.reshape(1, d), scale, shift)
    tm = _pick_tile(m, MM_ROW_TILE)
    tn = _pick_tile(n, COL_TILE)
    return pl.pallas_call(
        _matmul_kernel,
        grid=(m // tm, n // tn),
        in_specs=[
            pl.BlockSpec((tm, d), lambda i, j: (i, 0)),
            pl.BlockSpec((d, tn), lambda i, j: (0, j)),
        ],
        out_specs=pl.BlockSpec((tm, tn), lambda i, j: (i, j)),
        out_shape=jax.ShapeDtypeStruct((m, n), BF16),
        compiler_params=_cparams(2),
        name="in_proj",
    )(h, w)


def _out_kernel(a_ref, b_ref, wa_ref, wb_ref, x_ref, g_ref, o_ref):
    acc = _dot(a_ref[...], wa_ref[...]) + _dot(b_ref[...], wb_ref[...])
    o_ref[...] = x_ref[...] + g_ref[0] * acc


def _out_proj(mix_a, mix_b, w_out, x2, seq, gate):
    m, d = x2.shape
    wdt = mix_a.shape[1]
    nb = gate.shape[0]
    tm = _pick_tile(m, MM_ROW_TILE)
    tn = _pick_tile(d, COL_TILE)
    if nb == 1:
        cond_map = lambda i, j: (0, 0, j)
    else:
        assert seq % tm == 0
        cond_map = lambda i, j: ((i * tm) // seq, 0, j)
    return pl.pallas_call(
        _out_kernel,
        grid=(m // tm, d // tn),
        in_specs=[
            pl.BlockSpec((tm, wdt), lambda i, j: (i, 0)),
            pl.BlockSpec((tm, wdt), lambda i, j: (i, 0)),
            pl.BlockSpec((wdt, tn), lambda i, j: (0, j)),
            pl.BlockSpec((wdt, tn), lambda i, j: (1, j)),
            pl.BlockSpec((tm, tn), lambda i, j: (i, j)),
            pl.BlockSpec((1, 1, tn), cond_map),
        ],
        out_specs=pl.BlockSpec((tm, tn), lambda i, j: (i, j)),
        out_shape=jax.ShapeDtypeStruct((m, d), F32),
        compiler_params=_cparams(2),
        name="out_proj_residual",
    )(mix_a, mix_b, w_out, w_out, x2, gate)


def _attn_ctx0_kernel(q_ref, k_ref, v_ref, g_ref, qw_ref, kw_ref, o_ref, ko_ref, vo_ref):
    hd = LANES
    qscale = hd ** -0.5 * LOG2E
    scores = []
    for hh in range(q_ref.shape[-1] // hd):
        sl = slice(hh * hd, (hh + 1) * hd)
        qn = _rms(q_ref[:, sl].astype(F32), qw_ref[...]) * qscale
        kn = _rms(k_ref[:, sl].astype(F32), kw_ref[...])
        ko_ref[:, sl] = kn
        vo_ref[:, sl] = v_ref[:, sl].astype(F32)
        scores.append(_dot_nt(qn.astype(BF16), kn.astype(BF16)))
    for hh, s in enumerate(scores):
        sl = slice(hh * hd, (hh + 1) * hd)
        p = jnp.exp2(s - jnp.max(s, axis=-1, keepdims=True))
        l = jnp.sum(p, axis=-1, keepdims=True)
        o = _dot(p.astype(BF16), v_ref[:, sl]) / l
        o_ref[:, sl] = (o * _silu(g_ref[:, sl].astype(F32))).astype(o_ref.dtype)


def _attn_ctx0(proj, batch, seq, heads, q_norm, k_norm):
    hd = LANES
    m = batch * seq
    hp = CTX_ATTN_HEADS
    assert heads % hp == 0
    ng = heads // hp
    blk = lambda off: pl.BlockSpec((seq, hp * hd), lambda b, h: (b, off * ng + h))
    vec = pl.BlockSpec((1, hd), lambda b, h: (0, 0))
    return pl.pallas_call(
        _attn_ctx0_kernel,
        grid=(batch, ng),
        in_specs=[blk(0), blk(1), blk(2), blk(3), vec, vec],
        out_specs=[blk(0), blk(0), blk(0)],
        out_shape=[
            jax.ShapeDtypeStruct((m, heads * hd), BF16),
            jax.ShapeDtypeStruct((m, heads * hd), F32),
            jax.ShapeDtypeStruct((m, heads * hd), F32),
        ],
        compiler_params=_cparams(2),
        name="ctx_attention_l0",
    )(proj, proj, proj, proj, q_norm.reshape(1, hd), k_norm.reshape(1, hd))


def _na_bias_plan(rows):
    tiles = rows // NA_ROWS
    assert min(WIN_R, rows) == WIN_R and tiles >= 3
    t_var = np.array([0, 1, tiles - 1])
    ks = np.clip(NA_ROWS * t_var - WIN_R // 2, 0, rows - NA_SPAN)
    r = NA_ROWS * t_var[:, None] + np.arange(NA_ROWS)[None, :]
    rs = np.clip(r - WIN_R // 2, 0, rows - WIN_R)
    kr = ks[:, None] + np.arange(NA_SPAN)[None, :]
    row_ok = (kr[:, None, :] >= rs[:, :, None]) & (kr[:, None, :] < rs[:, :, None] + WIN_R)
    dr = kr[:, None, :] - r[:, :, None] + WIN_R - 1
    return np.where(row_ok, dr, -1)


def _na_bias_kernel(rpb_ref, o_ref, *, plan):
    h = pl.program_id(0)
    n_dr, n_dc = 2 * WIN_R - 1, 2 * WIN_C - 1
    qc = lax.broadcasted_iota(jnp.int32, (GRID_W, GRID_W), 0)
    kc = lax.broadcasted_iota(jnp.int32, (GRID_W, GRID_W), 1)
    cstart = jnp.clip(qc - WIN_C // 2, 0, GRID_W - WIN_C)
    col_ok = (kc >= cstart) & (kc < cstart + WIN_C)
    dc = jnp.clip(kc - qc + WIN_C - 1, 0, n_dc - 1)
    neg = jnp.full((GRID_W, GRID_W), NEG_INF, F32)
    slabs = {-1: neg}
    for dr in sorted(set(int(v) for v in plan.ravel()) - {-1}):
        acc = neg
        for j in range(n_dc):
            acc = jnp.where(col_ok & (dc == j), rpb_ref[(h * n_dr + dr) * n_dc + j] * LOG2E, acc)
        slabs[dr] = acc
    for var in range(plan.shape[0]):
        for i in range(plan.shape[1]):
            row = jnp.concatenate([slabs[int(v)] for v in plan[var, i]], axis=1)
            o_ref[0, var, i * GRID_W:(i + 1) * GRID_W, :] = row


def _na_bias_tables(rpb, rows):
    heads = rpb.shape[0]
    shape = (heads, 3, NA_ROWS * GRID_W, NA_SPAN * GRID_W)
    return pl.pallas_call(
        functools.partial(_na_bias_kernel, plan=_na_bias_plan(rows)),
        grid=(heads,),
        in_specs=[pl.BlockSpec(memory_space=pltpu.SMEM)],
        out_specs=pl.BlockSpec((1,) + shape[1:], lambda h: (h, 0, 0, 0)),
        out_shape=jax.ShapeDtypeStruct(shape, F32),
        compiler_params=_cparams(1),
        name="na_bias_tables",
    )(rpb.astype(F32).reshape(-1))


def _na_kernel(q_ref, k_ref, v_ref, g_ref, kc_ref, vc_ref, b_ref, qw_ref, kw_ref, o_ref, kn_ref, *, rows):
    t = pl.program_id(2)
    hd = LANES
    hp = q_ref.shape[-1] // hd

    @pl.when(t == 0)
    def _():
        for hh in range(hp):
            sl = slice(hh * hd, (hh + 1) * hd)
            kn_ref[:, sl] = _rms(k_ref[:, sl].astype(F32), kw_ref[...]).astype(BF16)

    ks = jnp.clip(NA_ROWS * t - WIN_R // 2, 0, rows - NA_SPAN)
    span = pl.ds(pl.multiple_of(ks * GRID_W, GRID_W), NA_SPAN * GRID_W)
    qscale = hd ** -0.5 * LOG2E
    scores = []
    for hh in range(hp):
        sl = slice(hh * hd, (hh + 1) * hd)
        qn = (_rms(q_ref[:, sl].astype(F32), qw_ref[...]) * qscale).astype(BF16)
        scores.append((_dot_nt(qn, kn_ref[span, sl]) + b_ref[hh, 0], _dot_nt(qn, kc_ref[0, :, sl])))
    for hh, (s_lat, s_ctx) in enumerate(scores):
        sl = slice(hh * hd, (hh + 1) * hd)
        mx = jnp.maximum(jnp.max(s_lat, axis=-1, keepdims=True), jnp.max(s_ctx, axis=-1, keepdims=True))
        p_lat = jnp.exp2(s_lat - mx)
        p_ctx = jnp.exp2(s_ctx - mx)
        l = jnp.sum(p_lat, axis=-1, keepdims=True) + jnp.sum(p_ctx, axis=-1, keepdims=True)
        o = (_dot(p_lat.astype(BF16), v_ref[span, sl]) + _dot(p_ctx.astype(BF16), vc_ref[0, :, sl])) / l
        o_ref[:, sl] = (o * _silu(g_ref[:, sl].astype(F32))).astype(o_ref.dtype)


def _na_latent(proj, batch, seq, heads, k_ctx, v_ctx, bias, q_norm, k_norm):
    hd = LANES
    rows = seq // GRID_W
    tiles = rows // NA_ROWS
    tq = NA_ROWS * GRID_W
    lc = k_ctx.shape[1]
    hp = min(NA_HEADS, heads)
    assert heads % hp == 0
    ng = heads // hp
    qblk = lambda off: pl.BlockSpec((tq, hp * hd), lambda b, h, t: (b * tiles + t, off * ng + h))
    full = lambda off: pl.BlockSpec((seq, hp * hd), lambda b, h, t: (b, off * ng + h))
    ctx = pl.BlockSpec((1, lc, hp * hd), lambda b, h, t: (b, 0, h))
    variant = lambda t: jnp.where(t == 0, 0, jnp.where(t == tiles - 1, 2, 1))
    vec = pl.BlockSpec((1, hd), lambda b, h, t: (0, 0))
    return pl.pallas_call(
        functools.partial(_na_kernel, rows=rows),
        grid=(batch, ng, tiles),
        in_specs=[
            qblk(0), full(1), full(2), qblk(3), ctx, ctx,
            pl.BlockSpec((hp, 1, tq, NA_SPAN * GRID_W), lambda b, h, t: (h, variant(t), 0, 0)),
            vec, vec,
        ],
        out_specs=qblk(0),
        out_shape=jax.ShapeDtypeStruct((batch * seq, heads * hd), BF16),
        scratch_shapes=[pltpu.VMEM((seq, hp * hd), BF16)],
        compiler_params=_cparams(3),
        name="neighbourhood_attention",
    )(proj, proj, proj, proj, k_ctx, v_ctx, bias, q_norm.reshape(1, hd), k_norm.reshape(1, hd))


def _sg_kernel(u_ref, v_ref, g_ref, nw_ref, w_ref, b_ref, o_ref):
    groups = w_ref.shape[0]
    hd = w_ref.shape[-1]
    u = _gelu(u_ref[...].astype(F32))
    v = _rms(_gelu(v_ref[...].astype(F32)), nw_ref[...]).astype(BF16)
    g = _silu(g_ref[...].astype(F32))
    for c in range(u_ref.shape[0] // SG_CHUNK):
        rows = slice(c * SG_CHUNK, (c + 1) * SG_CHUNK)
        for gi in range(groups):
            sl = slice(gi * hd, (gi + 1) * hd)
            mixed = _dot(w_ref[gi], v[rows, sl]) + b_ref[gi]
            o_ref[rows, sl] = (u[rows, sl] * mixed * g[rows, sl]).astype(o_ref.dtype)


def _spatial_gating(proj, width, sg_norm, sg_w, sg_b):
    m = proj.shape[0]
    groups = sg_w.shape[0]
    tr = SG_CHUNKS_PER_STEP * SG_CHUNK
    assert m % tr == 0
    blk = lambda off: pl.BlockSpec((tr, width), lambda i: (i, off))
    bias = jnp.broadcast_to(sg_b.astype(F32)[:, :, None], (groups, SG_CHUNK, LANES))
    return pl.pallas_call(
        _sg_kernel,
        grid=(m // tr,),
        in_specs=[
            blk(4), blk(5), blk(6),
            pl.BlockSpec((1, width), lambda i: (0, 0)),
            pl.BlockSpec((groups, SG_CHUNK, SG_CHUNK), lambda i: (0, 0, 0)),
            pl.BlockSpec((groups, SG_CHUNK, LANES), lambda i: (0, 0, 0)),
        ],
        out_specs=pl.BlockSpec((tr, width), lambda i: (i, 0)),
        out_shape=jax.ShapeDtypeStruct((m, width), BF16),
        compiler_params=_cparams(1),
        name="spatial_gating",
    )(proj, proj, proj, sg_norm.reshape(1, width), sg_w.astype(BF16), bias)


def _scan_matrix(rev):
    t = np.arange(SCAN_BLOCK)
    same = (t[:, None] // SCAN_CHUNK) == (t[None, :] // SCAN_CHUNK)
    incl = same & ((t[None, :] >= t[:, None]) if rev else (t[None, :] <= t[:, None]))
    return jnp.asarray(incl, dtype=BF16)


def _chunk_rows(a, pos):
    parts = []
    for c in range(a.shape[0] // SCAN_CHUNK):
        row = a[c * SCAN_CHUNK + pos:c * SCAN_CHUNK + pos + 1]
        parts.append(jnp.broadcast_to(row, (SCAN_CHUNK, a.shape[1])))
    return jnp.concatenate(parts, axis=0)


def _hgrn_block(q, z, v, lb, tri, states, rev):
    hd = LANES
    nc = q.shape[0] // SCAN_CHUNK
    half = SCAN_CHUNK // 2
    f = lb + (1.0 - lb) * jax.nn.sigmoid(z)
    g = jnp.log(f)
    k = 1.0 - f
    g_hi = g.astype(BF16)
    g_lo = (g - g_hi.astype(F32)).astype(BF16)
    a = _dot(tri, g_hi) + _dot(tri, g_lo)
    tot_row = 0 if rev else SCAN_CHUNK - 1
    a_ref = _chunk_rows(a, half if rev else half - 1)
    a_tot = _chunk_rows(a, tot_row)
    qe = (q * jnp.exp(jnp.minimum(a - a_ref, EXP_CLAMP))).astype(BF16)
    ke = (k * jnp.exp(jnp.minimum(a_ref - a, EXP_CLAMP))).astype(BF16)
    qa = (q * jnp.exp(a)).astype(BF16)
    kd = (k * jnp.exp(a_tot - a)).astype(BF16)
    heads = [slice(h * hd, (h + 1) * hd) for h in range(len(states))]
    chunks = [slice(c * SCAN_CHUNK, (c + 1) * SCAN_CHUNK) for c in range(nc)]
    dec = [jnp.exp(a[c * SCAN_CHUNK + tot_row:c * SCAN_CHUNK + tot_row + 1]) for c in range(nc)]
    probs = [_dot_nt(qe[:, sl], ke[:, sl]) for sl in heads]
    upd = [[_dot_tn(v[cs, sl], kd[cs, sl]) for cs in chunks] for sl in heads]
    in_chunk = tri > 0
    intra = [_dot(jnp.where(in_chunk, p.astype(BF16), 0.0), v[:, sl]) for p, sl in zip(probs, heads)]
    outs, new_states = [], []
    for h, sl in enumerate(heads):
        state = states[h]
        starts = [None] * nc
        for c in (range(nc - 1, -1, -1) if rev else range(nc)):
            starts[c] = state.astype(BF16)
            state = state * dec[c][:, sl] + upd[h][c]
        inter = [_dot_nt(qa[cs, sl], starts[c]) for c, cs in enumerate(chunks)]
        outs.append(intra[h] + jnp.concatenate(inter, axis=0))
        new_states.append(state)
    return outs, new_states


def _hgrn_kernel(*refs, rev, has_s0, emit_state, final):
    refs = list(refs)
    q_ref, z_ref, i_ref, lb_ref, tri_ref = refs[:5]
    del refs[:5]
    s0_ref = refs.pop(0) if has_s0 else None
    if final:
        of_ref, gate_ref, onw_ref = refs[:3]
        del refs[:3]
    o_ref = refs.pop(0)
    st_ref = refs.pop(0) if emit_state else None
    state_ref = refs.pop(0)
    t = pl.program_id(2)
    hd = LANES
    tri = tri_ref[...]

    hp = state_ref.shape[0]

    @pl.when(t == 0)
    def _():
        for hh in range(hp):
            state_ref[hh] = s0_ref[0, hh].T if has_s0 else jnp.zeros((hd, hd), F32)

    outs, states = _hgrn_block(q_ref[...].astype(F32), z_ref[...].astype(F32), i_ref[...], lb_ref[...], tri,
                               [state_ref[hh] for hh in range(hp)], rev)
    for hh in range(hp):
        sl = slice(hh * hd, (hh + 1) * hd)
        state_ref[hh] = states[hh]
        if final:
            o = _rms(outs[hh] + of_ref[:, sl], onw_ref[...])
            o_ref[:, sl] = (o * _silu(gate_ref[:, sl].astype(F32))).astype(o_ref.dtype)
        else:
            o_ref[:, sl] = outs[hh]

    if emit_state:
        @pl.when(t == pl.num_programs(2) - 1)
        def _():
            for hh in range(hp):
                st_ref[0, hh] = state_ref[hh].T


def _hgrn_pass(proj, batch, seq, heads, lb, s0, *, rev, emit_state, prev=None, out_norm=None):
    hd = LANES
    tb = _pick_tile(seq, SCAN_BLOCK)
    assert tb == SCAN_BLOCK
    nblk = seq // tb
    hp = min(SCAN_HEADS, heads)
    assert heads % hp == 0
    ng = heads // hp
    final = prev is not None
    tmap = (lambda t: nblk - 1 - t) if rev else (lambda t: t)
    blk = lambda off: pl.BlockSpec((tb, hp * hd), lambda b, h, t: (b * nblk + tmap(t), off * ng + h))
    in_specs = [
        blk(0), blk(2 if rev else 1), blk(3),
        pl.BlockSpec((1, hp * hd), lambda b, h, t: (0, h)),
        pl.BlockSpec((tb, tb), lambda b, h, t: (0, 0)),
    ]
    args = [proj, proj, proj, lb.reshape(1, heads * hd), _scan_matrix(rev)]
    if s0 is not None:
        in_specs.append(pl.BlockSpec((1, hp, hd, hd), lambda b, h, t: (b, h, 0, 0)))
        args.append(s0)
    if final:
        in_specs += [blk(0), blk(4), pl.BlockSpec((1, hd), lambda b, h, t: (0, 0))]
        args += [prev, proj, out_norm.reshape(1, hd)]
    out_specs = [blk(0)]
    out_shape = [jax.ShapeDtypeStruct((batch * seq, heads * hd), BF16 if final else F32)]
    if emit_state:
        out_specs.append(pl.BlockSpec((1, hp, hd, hd), lambda b, h, t: (b, h, 0, 0)))
        out_shape.append(jax.ShapeDtypeStruct((batch, heads, hd, hd), F32))
    res = pl.pallas_call(
        functools.partial(_hgrn_kernel, rev=rev, has_s0=s0 is not None, emit_state=emit_state, final=final),
        grid=(batch, ng, nblk),
        in_specs=in_specs,
        out_specs=out_specs,
        out_shape=out_shape,
        scratch_shapes=[pltpu.VMEM((hp, hd, hd), F32)],
        compiler_params=_cparams(3),
        name="hgrn_scan_bwd" if rev else "hgrn_scan_fwd",
    )(*args)
    return res if emit_state else (res[0], None)


_ROPE_SWAP = np.concatenate([np.arange(16, 32), np.arange(0, 16), np.arange(48, 64), np.arange(32, 48)])


def _rope_table(seq, rotate):
    if not rotate:
        row = np.concatenate([np.ones(ROPE_D), np.zeros(ROPE_D)]).astype(np.float32)
        return jnp.asarray(np.broadcast_to(row, (seq, 2 * ROPE_D)))
    t = jnp.arange(seq)
    half = ROPE_D // 4
    inv = ROPE_BASE ** (-jnp.arange(half, dtype=F32) / half)
    ang_r = (t // GRID_W).astype(F32)[:, None] * inv[None, :]
    ang_c = (t % GRID_W).astype(F32)[:, None] * inv[None, :]
    cos = jnp.concatenate([jnp.cos(ang_r)] * 2 + [jnp.cos(ang_c)] * 2, axis=-1)
    sin = jnp.concatenate([-jnp.sin(ang_r), jnp.sin(ang_r), -jnp.sin(ang_c), jnp.sin(ang_c)], axis=-1)
    return jnp.concatenate([cos, sin], axis=-1)


def _rotate_rope(r, tab, lane_lo):
    rot = r * tab
    rot = rot + pltpu.roll(rot, ROPE_D, 1)
    return jnp.where(lane_lo, rot, 0.0)


def _qup_kernel(cq_ref, nw_ref, w_ref, gn_ref, gr_ref, tab_ref, o_ref, h_ref):
    @pl.when(pl.program_id(1) == 0)
    def _():
        h_ref[...] = _rms(cq_ref[...].astype(F32), nw_ref[...]).astype(BF16)

    hw = NOPE_D + 2 * ROPE_D
    scale = (NOPE_D + ROPE_D) ** -0.5 * LOG2E
    lane_lo = lax.broadcasted_iota(jnp.int32, (1, LANES), 1) < ROPE_D
    tab = tab_ref[...] * gr_ref[...]
    y = _dot(h_ref[...], w_ref[...])
    ones_all = _ones_rows(LANES)
    ones_lo = _ones_rows(ROPE_D)
    gain_n = gn_ref[...] * scale
    for hh in range(y.shape[1] // hw):
        yn = y[:, hh * hw:hh * hw + NOPE_D]
        yr = y[:, hh * hw + NOPE_D:(hh + 1) * hw]
        o_ref[:, hh * hw:hh * hw + NOPE_D] = _head_rms(yn, gain_n, ones_all).astype(o_ref.dtype)
        ms_r = _dot((yr * yr).astype(BF16), ones_lo) * (1.0 / ROPE_D)
        qr = _rotate_rope(yr, tab, lane_lo) * (lax.rsqrt(ms_r + EPS) * scale)
        o_ref[:, hh * hw + NOPE_D:(hh + 1) * hw] = qr.astype(o_ref.dtype)


def _q_up(proj, col_blk, q_lora, seq, a_norm, w, gain_n, gain_r, table):
    m = proj.shape[0]
    n = w.shape[1]
    tm = _pick_tile(seq, ROW_TILE)
    tn = _pick_tile(n, 4 * COL_TILE)
    nt = table.shape[0] // tm
    return pl.pallas_call(
        _qup_kernel,
        grid=(m // tm, n // tn),
        in_specs=[
            pl.BlockSpec((tm, q_lora), lambda i, j: (i, col_blk)),
            pl.BlockSpec((1, q_lora), lambda i, j: (0, 0)),
            pl.BlockSpec((q_lora, tn), lambda i, j: (0, j)),
            pl.BlockSpec((1, NOPE_D), lambda i, j: (0, 0)),
            pl.BlockSpec((1, 2 * ROPE_D), lambda i, j: (0, 0)),
            pl.BlockSpec((tm, 2 * ROPE_D), lambda i, j: (i % nt, 0)),
        ],
        out_specs=pl.BlockSpec((tm, tn), lambda i, j: (i, j)),
        out_shape=jax.ShapeDtypeStruct((m, n), BF16),
        scratch_shapes=[pltpu.VMEM((tm, q_lora), BF16)],
        compiler_params=_cparams(2),
        name="mla_q_up",
    )(proj, a_norm.reshape(1, q_lora), w, gain_n.reshape(1, NOPE_D), gain_r.reshape(1, 2 * ROPE_D), table)


def _kvup_kernel(*refs, normalize_in, emit_cache, heads):
    refs = list(refs)
    ckv_ref, kr_ref, nw_ref, gkr_ref, w_ref, gk_ref, tab_ref, k_ref, v_ref = refs[:9]
    lane_lo = lax.broadcasted_iota(jnp.int32, (1, LANES), 1) < ROPE_D
    c = ckv_ref[...].astype(F32)
    r = kr_ref[...].astype(F32)
    if normalize_in:
        c = _rms(c, nw_ref[...])
        ms = jnp.sum(jnp.where(lane_lo, r * r, 0.0), axis=-1, keepdims=True) * (1.0 / ROPE_D)
        r = r * lax.rsqrt(ms + EPS) * gkr_ref[...]
    if emit_cache:
        refs[9][...] = c
        refs[10][...] = r[:, :ROPE_D]
    k_rope = _rotate_rope(r, tab_ref[...], lane_lo).astype(k_ref.dtype)
    y = _dot(c.astype(BF16), w_ref[...])
    hw = 2 * LANES
    for h in range(heads):
        k_ref[:, h * hw:h * hw + NOPE_D] = _rms(y[:, h * hw:h * hw + NOPE_D], gk_ref[...]).astype(k_ref.dtype)
        k_ref[:, h * hw + NOPE_D:(h + 1) * hw] = k_rope
        v_ref[:, h * LANES:(h + 1) * LANES] = y[:, h * hw + NOPE_D:(h + 1) * hw].astype(v_ref.dtype)


def _kv_up(ckv_src, ckv_blk, kr_src, kr_blk, seq, kv_lora, heads, a_norm, gain_kr, w, gain_k, table,
           *, normalize_in, emit_cache):
    m = ckv_src.shape[0]
    tm = _pick_tile(seq, ROW_TILE)
    nt = table.shape[0] // tm
    out_specs = [
        pl.BlockSpec((tm, heads * 2 * LANES), lambda i: (i, 0)),
        pl.BlockSpec((tm, heads * LANES), lambda i: (i, 0)),
    ]
    out_shape = [
        jax.ShapeDtypeStruct((m, heads * 2 * LANES), BF16),
        jax.ShapeDtypeStruct((m, heads * LANES), BF16),
    ]
    if emit_cache:
        out_specs += [pl.BlockSpec((tm, kv_lora), lambda i: (i, 0)), pl.BlockSpec((tm, ROPE_D), lambda i: (i, 0))]
        out_shape += [jax.ShapeDtypeStruct((m, kv_lora), F32), jax.ShapeDtypeStruct((m, ROPE_D), F32)]
    return pl.pallas_call(
        functools.partial(_kvup_kernel, normalize_in=normalize_in, emit_cache=emit_cache, heads=heads),
        grid=(m // tm,),
        in_specs=[
            pl.BlockSpec((tm, kv_lora), lambda i: (i, ckv_blk)),
            pl.BlockSpec((tm, 2 * ROPE_D), lambda i: (i, kr_blk)),
            pl.BlockSpec((1, kv_lora), lambda i: (0, 0)),
            pl.BlockSpec((1, 2 * ROPE_D), lambda i: (0, 0)),
            pl.BlockSpec((kv_lora, heads * 2 * LANES), lambda i: (0, 0)),
            pl.BlockSpec((1, NOPE_D), lambda i: (0, 0)),
            pl.BlockSpec((tm, 2 * ROPE_D), lambda i: (i % nt, 0)),
        ],
        out_specs=out_specs,
        out_shape=out_shape,
        compiler_params=_cparams(1),
        name="mla_kv_up",
    )(ckv_src, kr_src, a_norm.reshape(1, kv_lora), gain_kr.reshape(1, 2 * ROPE_D), w,
      gain_k.reshape(1, NOPE_D), table)


def _mla_attn_kernel(*refs, has_ctx):
    if has_ctx:
        q_ref, k_ref, v_ref, kc_ref, vc_ref, g_ref, o_ref, vt_ref, vct_ref = refs
    else:
        q_ref, k_ref, v_ref, g_ref, o_ref, vt_ref = refs
    qw = 2 * LANES
    hp = q_ref.shape[-1] // qw

    @pl.when(pl.program_id(2) == 0)
    def _():
        for hh in range(hp):
            sl = slice(hh * LANES, (hh + 1) * LANES)
            vt_ref[hh] = v_ref[:, sl].astype(F32).T.astype(BF16)
            if has_ctx:
                vct_ref[hh] = vc_ref[:, sl].astype(F32).T.astype(BF16)

    scores = []
    for hh in range(hp):
        q = q_ref[:, hh * qw:(hh + 1) * qw]
        s = _dot_nt(k_ref[:, hh * qw:(hh + 1) * qw], q)
        s_c = _dot_nt(kc_ref[:, hh * qw:(hh + 1) * qw], q) if has_ctx else None
        scores.append((s, s_c))
    for hh, (s, s_c) in enumerate(scores):
        sl = slice(hh * LANES, (hh + 1) * LANES)
        mx = jnp.max(s, axis=0, keepdims=True)
        if has_ctx:
            mx = jnp.maximum(mx, jnp.max(s_c, axis=0, keepdims=True))
        p = jnp.exp2(s - mx)
        l = jnp.sum(p, axis=0, keepdims=True)
        acc = _dot(vt_ref[hh], p.astype(BF16))
        if has_ctx:
            p_c = jnp.exp2(s_c - mx)
            l = l + jnp.sum(p_c, axis=0, keepdims=True)
            acc = acc + _dot(vct_ref[hh], p_c.astype(BF16))
        o = (acc / l).T
        o_ref[:, sl] = (o * _silu(g_ref[:, sl].astype(F32))).astype(o_ref.dtype)


def _mla_attention(q, k, v, proj, gate_blk, batch, seq, heads, hp, ctx=None):
    tq = _pick_tile(seq, ATTN_Q)
    nq = seq // tq
    assert heads % hp == 0 and gate_blk % hp == 0
    qmap = lambda b, h, t: (b * nq + t, h)
    in_specs = [
        pl.BlockSpec((tq, hp * 2 * LANES), qmap),
        pl.BlockSpec((seq, hp * 2 * LANES), lambda b, h, t: (b, h)),
        pl.BlockSpec((seq, hp * LANES), lambda b, h, t: (b, h)),
    ]
    args = [q, k, v]
    if ctx is not None:
        k_c, v_c = ctx
        lc = k_c.shape[0] // batch
        in_specs += [
            pl.BlockSpec((lc, hp * 2 * LANES), lambda b, h, t: (b, h)),
            pl.BlockSpec((lc, hp * LANES), lambda b, h, t: (b, h)),
        ]
        args += [k_c, v_c]
    in_specs.append(pl.BlockSpec((tq, hp * LANES), lambda b, h, t: (b * nq + t, gate_blk // hp + h)))
    args.append(proj)
    scratch = [pltpu.VMEM((hp, LANES, seq), BF16)]
    if ctx is not None:
        scratch.append(pltpu.VMEM((hp, LANES, lc), BF16))
    return pl.pallas_call(
        functools.partial(_mla_attn_kernel, has_ctx=ctx is not None),
        grid=(batch, heads // hp, nq),
        in_specs=in_specs,
        out_specs=pl.BlockSpec((tq, hp * LANES), qmap),
        out_shape=jax.ShapeDtypeStruct((batch * seq, heads * LANES), BF16),
        scratch_shapes=scratch,
        compiler_params=_cparams(3),
        name="mla_attention",
    )(*args)


def _relayout_kernel(w_ref, o_ref, *, plan):
    end = 0
    for dst, src, wd in plan:
        o_ref[:, dst:dst + wd] = w_ref[0, :, src:src + wd].astype(o_ref.dtype)
        end = max(end, dst + wd)
    if end < o_ref.shape[1]:
        o_ref[:, end:] = jnp.zeros((o_ref.shape[0], o_ref.shape[1] - end), o_ref.dtype)


def _relayout_columns(w, j, plan, n_out):
    _, k, n = w.shape
    tr = _pick_tile(k, RELAYOUT_ROWS)
    return pl.pallas_call(
        functools.partial(_relayout_kernel, plan=plan),
        grid=(k // tr,),
        in_specs=[pl.BlockSpec((1, tr, n), lambda i: (j, i, 0))],
        out_specs=pl.BlockSpec((tr, n_out), lambda i: (i, 0)),
        out_shape=jax.ShapeDtypeStruct((k, n_out), BF16),
        compiler_params=_cparams(1),
        name="weight_relayout",
    )(w)


def _pad_cols(w, mult):
    pad = (-w.shape[1]) % mult
    return w if pad == 0 else jnp.concatenate([w, jnp.zeros((w.shape[0], pad), w.dtype)], axis=1)


def kernel(x_prompt, x_sample, cache_na_k, cache_na_v, state_hgrn, cache_mla_ckv, cache_mla_krope, c, c_ctx,
           norm_w, w_ada, b_ada, w_out,
           w_in_ab, na_q_norm, na_k_norm, na_rpb, sg_norm, sg_w, sg_b,
           w_in_cd, hgrn_lb, hgrn_out_norm, mla_q_a_norm, mla_w_q_up, mla_kv_a_norm, mla_w_kv_up,
           mla_q_norm, mla_k_norm):
    bp, lp, d = x_prompt.shape
    bs, ls, _ = x_sample.shape
    depth = norm_w.shape[0]
    hd = cache_na_k.shape[-1]
    heads = cache_na_k.shape[-2]
    width = heads * hd
    assert hd == LANES and state_hgrn.shape[-2:] == (hd, hd) and ls % GRID_W == 0
    q_lora = mla_q_a_norm.shape[-1]
    kv_lora = mla_kv_a_norm.shape[-1]
    assert mla_q_norm.shape[-1] == NOPE_D + ROPE_D and cache_mla_krope.shape[-1] == ROPE_D
    assert mla_w_kv_up.shape[-1] == heads * 2 * LANES

    n_cond = bs + 1
    cond_rows = -(-n_cond // 8) * 8
    cond = jnp.concatenate([c, c_ctx[None, :], jnp.zeros((cond_rows - n_cond, d), F32)], axis=0)
    mod = _modulation(cond, w_ada, b_ada).reshape(depth, cond_rows, 3, 1, d)

    lb_soft = jax.nn.softmax(hgrn_lb.astype(F32), axis=0)
    lb_cum = jnp.cumsum(lb_soft, axis=0)
    lower_bounds = lb_cum - lb_cum[:1]

    xp = x_prompt.reshape(bp * lp, d)
    xs = x_sample.reshape(bs * ls, d)
    na_k_new, na_v_new, hgrn_new, ckv_new, kr_new = [], [], [], [], []
    for layer in range(depth):
        j = layer // 2
        shift_s, scale_s, gate_s = (mod[layer, :bs, i] for i in range(3))
        shift_p, scale_p, gate_p = (mod[layer, bs:bs + 1, i] for i in range(3))
        w_o = w_out[layer].astype(BF16)
        if layer % 2 == 0:
            w_in = _pad_cols(w_in_ab[j].astype(BF16), COL_TILE)
            proj_p = _in_proj(xp, lp, norm_w[layer], scale_p, shift_p, w_in)
            proj_s = _in_proj(xs, ls, norm_w[layer], scale_s, shift_s, w_in)
            mix_a_p, k_new, v_new = _attn_ctx0(proj_p, bp, lp, heads, na_q_norm[j], na_k_norm[j])
            na_k_new.append(k_new.reshape(bp, lp, heads, hd))
            na_v_new.append(v_new.reshape(bp, lp, heads, hd))
            bias = _na_bias_tables(na_rpb[j], ls // GRID_W)
            lc = cache_na_k.shape[2]
            mix_a_s = _na_latent(proj_s, bs, ls, heads,
                                 cache_na_k[:, j].reshape(bs, lc, width).astype(BF16),
                                 cache_na_v[:, j].reshape(bs, lc, width).astype(BF16),
                                 bias, na_q_norm[j], na_k_norm[j])
            mix_b_p = _spatial_gating(proj_p, width, sg_norm[j], sg_w[j], sg_b[j])
            mix_b_s = _spatial_gating(proj_s, width, sg_norm[j], sg_w[j], sg_b[j])
        else:
            kr_src = 5 * width + q_lora + kv_lora
            gd_src = kr_src + ROPE_D
            kr_dst = 6 * width + q_lora + kv_lora
            half = ROPE_D // 4
            plan = [(0, 0, 5 * width), (5 * width, gd_src, width), (6 * width, 5 * width, q_lora + kv_lora),
                    (kr_dst, kr_src, ROPE_D)]
            plan += [(kr_dst + ROPE_D + i * half, kr_src + int(_ROPE_SWAP[i * half]), half) for i in range(4)]
            w_in = _relayout_columns(w_in_cd, j, plan, -(-(kr_dst + 2 * ROPE_D) // COL_TILE) * COL_TILE)
            assert (6 * width) % q_lora == 0 and (6 * width + q_lora) % kv_lora == 0
            cq_blk = (6 * width) // q_lora
            ckv_blk = (6 * width + q_lora) // kv_lora
            kr_blk = (6 * width + q_lora + kv_lora) // LANES
            gd_blk = (5 * width) // LANES
            proj_p = _in_proj(xp, lp, norm_w[layer], scale_p, shift_p, w_in)
            proj_s = _in_proj(xs, ls, norm_w[layer], scale_s, shift_s, w_in)

            lb_f, lb_b = lower_bounds[layer, 0], lower_bounds[layer, 1]
            of_p, st_f = _hgrn_pass(proj_p, bp, lp, heads, lb_f, None, rev=False, emit_state=True)
            mix_a_p, st_b = _hgrn_pass(proj_p, bp, lp, heads, lb_b, None, rev=True, emit_state=True,
                                       prev=of_p, out_norm=hgrn_out_norm[j])
            hgrn_new.append(jnp.stack([st_f, st_b], axis=1))
            of_s, _ = _hgrn_pass(proj_s, bs, ls, heads, lb_f, state_hgrn[:, j, 0], rev=False, emit_state=False)
            mix_a_s, _ = _hgrn_pass(proj_s, bs, ls, heads, lb_b, state_hgrn[:, j, 1], rev=True, emit_state=False,
                                    prev=of_s, out_norm=hgrn_out_norm[j])

            wq = mla_w_q_up[j].reshape(q_lora, heads, NOPE_D + ROPE_D)
            wq = jnp.concatenate([wq, wq[:, :, NOPE_D:][:, :, _ROPE_SWAP]], axis=-1)
            wq = wq.reshape(q_lora, heads * 2 * LANES).astype(BF16)
            wkv = mla_w_kv_up[j].astype(BF16)
            gq, gk = mla_q_norm[j], mla_k_norm[j]
            gq_r = jnp.concatenate([gq[NOPE_D:], gq[NOPE_D:][_ROPE_SWAP]])
            gk_r = jnp.concatenate([gk[NOPE_D:], gk[NOPE_D:][_ROPE_SWAP]])
            tab_p = _rope_table(_pick_tile(lp, ROW_TILE), False)
            tab_s = _rope_table(ls, True)
            tab_c = _rope_table(_pick_tile(cache_mla_ckv.shape[2], ROW_TILE), False)

            q_p = _q_up(proj_p, cq_blk, q_lora, lp, mla_q_a_norm[j], wq, gq[:NOPE_D], gq_r, tab_p)
            k_p, v_p, ckv_n, kr_n = _kv_up(proj_p, ckv_blk, proj_p, kr_blk, lp, kv_lora, heads, mla_kv_a_norm[j],
                                           gk_r, wkv, gk[:NOPE_D], tab_p, normalize_in=True, emit_cache=True)
            ckv_new.append(ckv_n.reshape(bp, lp, kv_lora))
            kr_new.append(kr_n.reshape(bp, lp, ROPE_D))
            mix_b_p = _mla_attention(q_p, k_p, v_p, proj_p, gd_blk, bp, lp, heads, CTX_ATTN_HEADS)

            q_s = _q_up(proj_s, cq_blk, q_lora, ls, mla_q_a_norm[j], wq, gq[:NOPE_D], gq_r, tab_s)
            k_s, v_s = _kv_up(proj_s, ckv_blk, proj_s, kr_blk, ls, kv_lora, heads, mla_kv_a_norm[j],
                              gk_r, wkv, gk[:NOPE_D], tab_s, normalize_in=True, emit_cache=False)
            lc = cache_mla_ckv.shape[2]
            kr_c = cache_mla_krope[:, j].reshape(bs * lc, ROPE_D)
            kr_c = jnp.concatenate([kr_c, jnp.zeros_like(kr_c)], axis=1)
            k_c, v_c = _kv_up(cache_mla_ckv[:, j].reshape(bs * lc, kv_lora), 0, kr_c, 0, lc, kv_lora, heads,
                              mla_kv_a_norm[j], gk_r, wkv, gk[:NOPE_D], tab_c, normalize_in=False, emit_cache=False)
            mix_b_s = _mla_attention(q_s, k_s, v_s, proj_s, gd_blk, bs, ls, heads, LAT_ATTN_HEADS, ctx=(k_c, v_c))
        xp = _out_proj(mix_a_p, mix_b_p, w_o, xp, lp, gate_p)
        xs = _out_proj(mix_a_s, mix_b_s, w_o, xs, ls, gate_s)
    return (xp.reshape(bp, lp, d), xs.reshape(bs, ls, d),
            jnp.stack(na_k_new, axis=1), jnp.stack(na_v_new, axis=1), jnp.stack(hgrn_new, axis=1),
            jnp.stack(ckv_new, axis=1), jnp.stack(kr_new, axis=1))
```

```python
import functools

import numpy as np
import jax
import jax.numpy as jnp
from jax import lax
from jax.experimental import pallas as pl
from jax.experimental.pallas import tpu as pltpu

F32 = jnp.float32
BF16 = jnp.bfloat16

GRID_W = 64
WIN_R = 8
WIN_C = 16
SG_CHUNK = 128
NOPE_D = 128
ROPE_D = 64
ROPE_BASE = 10000.0
EPS = 1e-6
NEG_INF = -1e30

LANES = 128
VMEM_LIMIT = 56 * 1024 * 1024

ROW_TILE = 512
MM_ROW_TILE = 1024
COL_TILE = 1024
NA_ROWS = 4
NA_SPAN = NA_ROWS + WIN_R
ATTN_Q = 512
SCAN_BLOCK = 256
SCAN_CHUNK = 32
SCAN_HEADS = 16
CTX_ATTN_HEADS = 16
LAT_ATTN_HEADS = 2
NA_HEADS = 4
SG_CHUNKS_PER_STEP = 4
LOG2E = 1.4426950408889634
RELAYOUT_ROWS = 256
EXP_CLAMP = 80.0


def _cparams(n_axes, **extra):
    return pltpu.CompilerParams(
        dimension_semantics=("arbitrary",) * n_axes, vmem_limit_bytes=VMEM_LIMIT, **extra)


def _pick_tile(n, pref):
    t = pref
    while n % t:
        t //= 2
    assert t >= LANES or t == n, (n, pref)
    return t


def _rms(x, w):
    return x * lax.rsqrt(jnp.mean(x * x, axis=-1, keepdims=True) + EPS) * w


def _ones_rows(n):
    row = lax.broadcasted_iota(jnp.int32, (LANES, LANES), 0)
    return jnp.where(row < n, 1.0, 0.0).astype(BF16)


def _head_rms(x, w, ones, n=LANES):
    ms = _dot((x * x).astype(BF16), ones) * (1.0 / n)
    return x * lax.rsqrt(ms + EPS) * w


def _silu(x):
    return x * jax.nn.sigmoid(x)


def _gelu(x):
    return 0.5 * x * (1.0 + jnp.tanh(0.7978845608028654 * (x + 0.044715 * (x * x * x))))


def _dot(a, b):
    return jnp.dot(a, b, preferred_element_type=F32)


def _dot_nt(a, b):
    return lax.dot_general(a, b, (((1,), (1,)), ((), ())), preferred_element_type=F32)


def _dot_tn(a, b):
    return lax.dot_general(a, b, (((0,), (0,)), ((), ())), preferred_element_type=F32)


def _mod_kernel(c_ref, w_ref, b_ref, o_ref):
    a = _silu(c_ref[...]).astype(BF16)
    o_ref[0] = _dot(a, w_ref[0].astype(BF16)) + b_ref[0]


def _modulation(cond, w_ada, b_ada):
    depth, d, n = w_ada.shape
    r = cond.shape[0]
    tn = _pick_tile(n, 512)
    return pl.pallas_call(
        _mod_kernel,
        grid=(depth, n // tn),
        in_specs=[
            pl.BlockSpec((r, d), lambda l, j: (0, 0)),
            pl.BlockSpec((1, d, tn), lambda l, j: (l, 0, j)),
            pl.BlockSpec((1, 1, tn), lambda l, j: (l, 0, j)),
        ],
        out_specs=pl.BlockSpec((1, r, tn), lambda l, j: (l, 0, j)),
        out_shape=jax.ShapeDtypeStruct((depth, r, n), F32),
        compiler_params=_cparams(2),
        name="adaln_modulation",
    )(cond, w_ada, b_ada.reshape(depth, 1, n))


def _norm_mod_kernel(x_ref, nw_ref, sc_ref, sh_ref, o_ref):
    h = _rms(x_ref[...], nw_ref[...])
    o_ref[...] = (h * (1.0 + sc_ref[0]) + sh_ref[0]).astype(o_ref.dtype)


def _matmul_kernel(a_ref, w_ref, o_ref):
    o_ref[...] = _dot(a_ref[...], w_ref[...]).astype(o_ref.dtype)


def _in_proj(x2, seq, norm_w, scale, shift, w):
    m, d = x2.shape
    n = w.shape[1]
    nb = scale.shape[0]
    tr = _pick_tile(m, ROW_TILE)
    if nb == 1:
        cond_map = lambda i: (0, 0, 0)
    else:
        assert seq % tr == 0
        cond_map = lambda i: ((i * tr) // seq, 0, 0)
    h = pl.pallas_call(
        _norm_mod_kernel,
        grid=(m // tr,),
        in_specs=[
            pl.BlockSpec((tr, d), lambda i: (i, 0)),
            pl.BlockSpec((1, d), lambda i: (0, 0)),
            pl.BlockSpec((1, 1, d), cond_map),
            pl.BlockSpec((1, 1, d), cond_map),
        ],
        out_specs=pl.BlockSpec((tr, d), lambda i: (i, 0)),
        out_shape=jax.ShapeDtypeStruct((m, d), BF16),
        compiler_params=_cparams(1),
        name="norm_modulate",
    )(x2, norm_w.reshape(1, d), scale, shift)
    tm = _pick_tile(m, MM_ROW_TILE)
    tn = _pick_tile(n, COL_TILE)
    return pl.pallas_call(
        _matmul_kernel,
        grid=(m // tm, n // tn),
        in_specs=[
            pl.BlockSpec((tm, d), lambda i, j: (i, 0)),
            pl.BlockSpec((d, tn), lambda i, j: (0, j)),
        ],
        out_specs=pl.BlockSpec((tm, tn), lambda i, j: (i, j)),
        out_shape=jax.ShapeDtypeStruct((m, n), BF16),
        compiler_params=_cparams(2),
        name="in_proj",
    )(h, w)


def _out_kernel(a_ref, b_ref, wa_ref, wb_ref, x_ref, g_ref, o_ref):
    acc = _dot(a_ref[...], wa_ref[...]) + _dot(b_ref[...], wb_ref[...])
    o_ref[...] = x_ref[...] + g_ref[0] * acc


def _out_proj(mix_a, mix_b, w_out, x2, seq, gate):
    m, d = x2.shape
    wdt = mix_a.shape[1]
    nb = gate.shape[0]
    tm = _pick_tile(m, MM_ROW_TILE)
    tn = _pick_tile(d, COL_TILE)
    if nb == 1:
        cond_map = lambda i, j: (0, 0, j)
    else:
        assert seq % tm == 0
        cond_map = lambda i, j: ((i * tm) // seq, 0, j)
    return pl.pallas_call(
        _out_kernel,
        grid=(m // tm, d // tn),
        in_specs=[
            pl.BlockSpec((tm, wdt), lambda i, j: (i, 0)),
            pl.BlockSpec((tm, wdt), lambda i, j: (i, 0)),
            pl.BlockSpec((wdt, tn), lambda i, j: (0, j)),
            pl.BlockSpec((wdt, tn), lambda i, j: (1, j)),
            pl.BlockSpec((tm, tn), lambda i, j: (i, j)),
            pl.BlockSpec((1, 1, tn), cond_map),
        ],
        out_specs=pl.BlockSpec((tm, tn), lambda i, j: (i, j)),
        out_shape=jax.ShapeDtypeStruct((m, d), F32),
        compiler_params=_cparams(2),
        name="out_proj_residual",
    )(mix_a, mix_b, w_out, w_out, x2, gate)


def _attn_ctx0_kernel(q_ref, k_ref, v_ref, g_ref, qw_ref, kw_ref, o_ref, ko_ref, vo_ref):
    hd = LANES
    qscale = hd ** -0.5 * LOG2E
    scores = []
    for hh in range(q_ref.shape[-1] // hd):
        sl = slice(hh * hd, (hh + 1) * hd)
        qn = _rms(q_ref[:, sl].astype(F32), qw_ref[...]) * qscale
        kn = _rms(k_ref[:, sl].astype(F32), kw_ref[...])
        ko_ref[:, sl] = kn
        vo_ref[:, sl] = v_ref[:, sl].astype(F32)
        scores.append(_dot_nt(qn.astype(BF16), kn.astype(BF16)))
    for hh, s in enumerate(scores):
        sl = slice(hh * hd, (hh + 1) * hd)
        p = jnp.exp2(s - jnp.max(s, axis=-1, keepdims=True))
        l = jnp.sum(p, axis=-1, keepdims=True)
        o = _dot(p.astype(BF16), v_ref[:, sl]) / l
        o_ref[:, sl] = (o * _silu(g_ref[:, sl].astype(F32))).astype(o_ref.dtype)


def _attn_ctx0(proj, batch, seq, heads, q_norm, k_norm):
    hd = LANES
    m = batch * seq
    hp = CTX_ATTN_HEADS
    assert heads % hp == 0
    ng = heads // hp
    blk = lambda off: pl.BlockSpec((seq, hp * hd), lambda b, h: (b, off * ng + h))
    vec = pl.BlockSpec((1, hd), lambda b, h: (0, 0))
    return pl.pallas_call(
        _attn_ctx0_kernel,
        grid=(batch, ng),
        in_specs=[blk(0), blk(1), blk(2), blk(3), vec, vec],
        out_specs=[blk(0), blk(0), blk(0)],
        out_shape=[
            jax.ShapeDtypeStruct((m, heads * hd), BF16),
            jax.ShapeDtypeStruct((m, heads * hd), F32),
            jax.ShapeDtypeStruct((m, heads * hd), F32),
        ],
        compiler_params=_cparams(2),
        name="ctx_attention_l0",
    )(proj, proj, proj, proj, q_norm.reshape(1, hd), k_norm.reshape(1, hd))


def _na_bias_plan(rows):
    tiles = rows // NA_ROWS
    assert min(WIN_R, rows) == WIN_R and tiles >= 3
    t_var = np.array([0, 1, tiles - 1])
    ks = np.clip(NA_ROWS * t_var - WIN_R // 2, 0, rows - NA_SPAN)
    r = NA_ROWS * t_var[:, None] + np.arange(NA_ROWS)[None, :]
    rs = np.clip(r - WIN_R // 2, 0, rows - WIN_R)
    kr = ks[:, None] + np.arange(NA_SPAN)[None, :]
    row_ok = (kr[:, None, :] >= rs[:, :, None]) & (kr[:, None, :] < rs[:, :, None] + WIN_R)
    dr = kr[:, None, :] - r[:, :, None] + WIN_R - 1
    return np.where(row_ok, dr, -1)


def _na_bias_kernel(rpb_ref, o_ref, *, plan):
    h = pl.program_id(0)
    n_dr, n_dc = 2 * WIN_R - 1, 2 * WIN_C - 1
    qc = lax.broadcasted_iota(jnp.int32, (GRID_W, GRID_W), 0)
    kc = lax.broadcasted_iota(jnp.int32, (GRID_W, GRID_W), 1)
    cstart = jnp.clip(qc - WIN_C // 2, 0, GRID_W - WIN_C)
    col_ok = (kc >= cstart) & (kc < cstart + WIN_C)
    dc = jnp.clip(kc - qc + WIN_C - 1, 0, n_dc - 1)
    neg = jnp.full((GRID_W, GRID_W), NEG_INF, F32)
    slabs = {-1: neg}
    for dr in sorted(set(int(v) for v in plan.ravel()) - {-1}):
        acc = neg
        for j in range(n_dc):
            acc = jnp.where(col_ok & (dc == j), rpb_ref[(h * n_dr + dr) * n_dc + j] * LOG2E, acc)
        slabs[dr] = acc
    for var in range(plan.shape[0]):
        for i in range(plan.shape[1]):
            row = jnp.concatenate([slabs[int(v)] for v in plan[var, i]], axis=1)
            o_ref[0, var, i * GRID_W:(i + 1) * GRID_W, :] = row


def _na_bias_tables(rpb, rows):
    heads = rpb.shape[0]
    shape = (heads, 3, NA_ROWS * GRID_W, NA_SPAN * GRID_W)
    return pl.pallas_call(
        functools.partial(_na_bias_kernel, plan=_na_bias_plan(rows)),
        grid=(heads,),
        in_specs=[pl.BlockSpec(memory_space=pltpu.SMEM)],
        out_specs=pl.BlockSpec((1,) + shape[1:], lambda h: (h, 0, 0, 0)),
        out_shape=jax.ShapeDtypeStruct(shape, F32),
        compiler_params=_cparams(1),
        name="na_bias_tables",
    )(rpb.astype(F32).reshape(-1))


def _na_kernel(q_ref, k_ref, v_ref, g_ref, kc_ref, vc_ref, b_ref, qw_ref, kw_ref, o_ref, kn_ref, *, rows):
    t = pl.program_id(2)
    hd = LANES
    hp = q_ref.shape[-1] // hd

    @pl.when(t == 0)
    def _():
        for hh in range(hp):
            sl = slice(hh * hd, (hh + 1) * hd)
            kn_ref[:, sl] = _rms(k_ref[:, sl].astype(F32), kw_ref[...]).astype(BF16)

    ks = jnp.clip(NA_ROWS * t - WIN_R // 2, 0, rows - NA_SPAN)
    span = pl.ds(pl.multiple_of(ks * GRID_W, GRID_W), NA_SPAN * GRID_W)
    qscale = hd ** -0.5 * LOG2E
    scores = []
    for hh in range(hp):
        sl = slice(hh * hd, (hh + 1) * hd)
        qn = (_rms(q_ref[:, sl].astype(F32), qw_ref[...]) * qscale).astype(BF16)
        scores.append((_dot_nt(qn, kn_ref[span, sl]) + b_ref[hh, 0], _dot_nt(qn, kc_ref[0, :, sl])))
    for hh, (s_lat, s_ctx) in enumerate(scores):
        sl = slice(hh * hd, (hh + 1) * hd)
        mx = jnp.maximum(jnp.max(s_lat, axis=-1, keepdims=True), jnp.max(s_ctx, axis=-1, keepdims=True))
        p_lat = jnp.exp2(s_lat - mx)
        p_ctx = jnp.exp2(s_ctx - mx)
        l = jnp.sum(p_lat, axis=-1, keepdims=True) + jnp.sum(p_ctx, axis=-1, keepdims=True)
        o = (_dot(p_lat.astype(BF16), v_ref[span, sl]) + _dot(p_ctx.astype(BF16), vc_ref[0, :, sl])) / l
        o_ref[:, sl] = (o * _silu(g_ref[:, sl].astype(F32))).astype(o_ref.dtype)


def _na_latent(proj, batch, seq, heads, k_ctx, v_ctx, bias, q_norm, k_norm):
    hd = LANES
    rows = seq // GRID_W
    tiles = rows // NA_ROWS
    tq = NA_ROWS * GRID_W
    lc = k_ctx.shape[1]
    hp = min(NA_HEADS, heads)
    assert heads % hp == 0
    ng = heads // hp
    qblk = lambda off: pl.BlockSpec((tq, hp * hd), lambda b, h, t: (b * tiles + t, off * ng + h))
    full = lambda off: pl.BlockSpec((seq, hp * hd), lambda b, h, t: (b, off * ng + h))
    ctx = pl.BlockSpec((1, lc, hp * hd), lambda b, h, t: (b, 0, h))
    variant = lambda t: jnp.where(t == 0, 0, jnp.where(t == tiles - 1, 2, 1))
    vec = pl.BlockSpec((1, hd), lambda b, h, t: (0, 0))
    return pl.pallas_call(
        functools.partial(_na_kernel, rows=rows),
        grid=(batch, ng, tiles),
        in_specs=[
            qblk(0), full(1), full(2), qblk(3), ctx, ctx,
            pl.BlockSpec((hp, 1, tq, NA_SPAN * GRID_W), lambda b, h, t: (h, variant(t), 0, 0)),
            vec, vec,
        ],
        out_specs=qblk(0),
        out_shape=jax.ShapeDtypeStruct((batch * seq, heads * hd), BF16),
        scratch_shapes=[pltpu.VMEM((seq, hp * hd), BF16)],
        compiler_params=_cparams(3),
        name="neighbourhood_attention",
    )(proj, proj, proj, proj, k_ctx, v_ctx, bias, q_norm.reshape(1, hd), k_norm.reshape(1, hd))


def _sg_kernel(u_ref, v_ref, g_ref, nw_ref, w_ref, b_ref, o_ref):
    groups = w_ref.shape[0]
    hd = w_ref.shape[-1]
    u = _gelu(u_ref[...].astype(F32))
    v = _rms(_gelu(v_ref[...].astype(F32)), nw_ref[...]).astype(BF16)
    g = _silu(g_ref[...].astype(F32))
    for c in range(u_ref.shape[0] // SG_CHUNK):
        rows = slice(c * SG_CHUNK, (c + 1) * SG_CHUNK)
        for gi in range(groups):
            sl = slice(gi * hd, (gi + 1) * hd)
            mixed = _dot(w_ref[gi], v[rows, sl]) + b_ref[gi]
            o_ref[rows, sl] = (u[rows, sl] * mixed * g[rows, sl]).astype(o_ref.dtype)


def _spatial_gating(proj, width, sg_norm, sg_w, sg_b):
    m = proj.shape[0]
    groups = sg_w.shape[0]
    tr = SG_CHUNKS_PER_STEP * SG_CHUNK
    assert m % tr == 0
    blk = lambda off: pl.BlockSpec((tr, width), lambda i: (i, off))
    bias = jnp.broadcast_to(sg_b.astype(F32)[:, :, None], (groups, SG_CHUNK, LANES))
    return pl.pallas_call(
        _sg_kernel,
        grid=(m // tr,),
        in_specs=[
            blk(4), blk(5), blk(6),
            pl.BlockSpec((1, width), lambda i: (0, 0)),
            pl.BlockSpec((groups, SG_CHUNK, SG_CHUNK), lambda i: (0, 0, 0)),
            pl.BlockSpec((groups, SG_CHUNK, LANES), lambda i: (0, 0, 0)),
        ],
        out_specs=pl.BlockSpec((tr, width), lambda i: (i, 0)),
        out_shape=jax.ShapeDtypeStruct((m, width), BF16),
        compiler_params=_cparams(1),
        name="spatial_gating",
    )(proj, proj, proj, sg_norm.reshape(1, width), sg_w.astype(BF16), bias)


def _scan_matrix(rev):
    t = np.arange(SCAN_BLOCK)
    same = (t[:, None] // SCAN_CHUNK) == (t[None, :] // SCAN_CHUNK)
    incl = same & ((t[None, :] >= t[:, None]) if rev else (t[None, :] <= t[:, None]))
    return jnp.asarray(incl, dtype=BF16)


def _chunk_rows(a, pos):
    parts = []
    for c in range(a.shape[0] // SCAN_CHUNK):
        row = a[c * SCAN_CHUNK + pos:c * SCAN_CHUNK + pos + 1]
        parts.append(jnp.broadcast_to(row, (SCAN_CHUNK, a.shape[1])))
    return jnp.concatenate(parts, axis=0)


def _hgrn_block(q, z, v, lb, tri, states, rev):
    hd = LANES
    nc = q.shape[0] // SCAN_CHUNK
    half = SCAN_CHUNK // 2
    f = lb + (1.0 - lb) * jax.nn.sigmoid(z)
    g = jnp.log(f)
    k = 1.0 - f
    g_hi = g.astype(BF16)
    g_lo = (g - g_hi.astype(F32)).astype(BF16)
    a = _dot(tri, g_hi) + _dot(tri, g_lo)
    tot_row = 0 if rev else SCAN_CHUNK - 1
    a_ref = _chunk_rows(a, half if rev else half - 1)
    a_tot = _chunk_rows(a, tot_row)
    qe = (q * jnp.exp(jnp.minimum(a - a_ref, EXP_CLAMP))).astype(BF16)
    ke = (k * jnp.exp(jnp.minimum(a_ref - a, EXP_CLAMP))).astype(BF16)
    qa = (q * jnp.exp(a)).astype(BF16)
    kd = (k * jnp.exp(a_tot - a)).astype(BF16)
    heads = [slice(h * hd, (h + 1) * hd) for h in range(len(states))]
    chunks = [slice(c * SCAN_CHUNK, (c + 1) * SCAN_CHUNK) for c in range(nc)]
    dec = [jnp.exp(a[c * SCAN_CHUNK + tot_row:c * SCAN_CHUNK + tot_row + 1]) for c in range(nc)]
    probs = [_dot_nt(qe[:, sl], ke[:, sl]) for sl in heads]
    upd = [[_dot_tn(v[cs, sl], kd[cs, sl]) for cs in chunks] for sl in heads]
    in_chunk = tri > 0
    intra = [_dot(jnp.where(in_chunk, p.astype(BF16), 0.0), v[:, sl]) for p, sl in zip(probs, heads)]
    outs, new_states = [], []
    for h, sl in enumerate(heads):
        state = states[h]
        starts = [None] * nc
        for c in (range(nc - 1, -1, -1) if rev else range(nc)):
            starts[c] = state.astype(BF16)
            state = state * dec[c][:, sl] + upd[h][c]
        inter = [_dot_nt(qa[cs, sl], starts[c]) for c, cs in enumerate(chunks)]
        outs.append(intra[h] + jnp.concatenate(inter, axis=0))
        new_states.append(state)
    return outs, new_states


def _hgrn_kernel(*refs, rev, has_s0, emit_state, final):
    refs = list(refs)
    q_ref, z_ref, i_ref, lb_ref, tri_ref = refs[:5]
    del refs[:5]
    s0_ref = refs.pop(0) if has_s0 else None
    if final:
        of_ref, gate_ref, onw_ref = refs[:3]
        del refs[:3]
    o_ref = refs.pop(0)
    st_ref = refs.pop(0) if emit_state else None
    state_ref = refs.pop(0)
    t = pl.program_id(2)
    hd = LANES
    tri = tri_ref[...]

    hp = state_ref.shape[0]

    @pl.when(t == 0)
    def _():
        for hh in range(hp):
            state_ref[hh] = s0_ref[0, hh].T if has_s0 else jnp.zeros((hd, hd), F32)

    outs, states = _hgrn_block(q_ref[...].astype(F32), z_ref[...].astype(F32), i_ref[...], lb_ref[...], tri,
                               [state_ref[hh] for hh in range(hp)], rev)
    for hh in range(hp):
        sl = slice(hh * hd, (hh + 1) * hd)
        state_ref[hh] = states[hh]
        if final:
            o = _rms(outs[hh] + of_ref[:, sl], onw_ref[...])
            o_ref[:, sl] = (o * _silu(gate_ref[:, sl].astype(F32))).astype(o_ref.dtype)
        else:
            o_ref[:, sl] = outs[hh]

    if emit_state:
        @pl.when(t == pl.num_programs(2) - 1)
        def _():
            for hh in range(hp):
                st_ref[0, hh] = state_ref[hh].T


def _hgrn_pass(proj, batch, seq, heads, lb, s0, *, rev, emit_state, prev=None, out_norm=None):
    hd = LANES
    tb = _pick_tile(seq, SCAN_BLOCK)
    assert tb == SCAN_BLOCK
    nblk = seq // tb
    hp = min(SCAN_HEADS, heads)
    assert heads % hp == 0
    ng = heads // hp
    final = prev is not None
    tmap = (lambda t: nblk - 1 - t) if rev else (lambda t: t)
    blk = lambda off: pl.BlockSpec((tb, hp * hd), lambda b, h, t: (b * nblk + tmap(t), off * ng + h))
    in_specs = [
        blk(0), blk(2 if rev else 1), blk(3),
        pl.BlockSpec((1, hp * hd), lambda b, h, t: (0, h)),
        pl.BlockSpec((tb, tb), lambda b, h, t: (0, 0)),
    ]
    args = [proj, proj, proj, lb.reshape(1, heads * hd), _scan_matrix(rev)]
    if s0 is not None:
        in_specs.append(pl.BlockSpec((1, hp, hd, hd), lambda b, h, t: (b, h, 0, 0)))
        args.append(s0)
    if final:
        in_specs += [blk(0), blk(4), pl.BlockSpec((1, hd), lambda b, h, t: (0, 0))]
        args += [prev, proj, out_norm.reshape(1, hd)]
    out_specs = [blk(0)]
    out_shape = [jax.ShapeDtypeStruct((batch * seq, heads * hd), BF16 if final else F32)]
    if emit_state:
        out_specs.append(pl.BlockSpec((1, hp, hd, hd), lambda b, h, t: (b, h, 0, 0)))
        out_shape.append(jax.ShapeDtypeStruct((batch, heads, hd, hd), F32))
    res = pl.pallas_call(
        functools.partial(_hgrn_kernel, rev=rev, has_s0=s0 is not None, emit_state=emit_state, final=final),
        grid=(batch, ng, nblk),
        in_specs=in_specs,
        out_specs=out_specs,
        out_shape=out_shape,
        scratch_shapes=[pltpu.VMEM((hp, hd, hd), F32)],
        compiler_params=_cparams(3),
        name="hgrn_scan_bwd" if rev else "hgrn_scan_fwd",
    )(*args)
    return res if emit_state else (res[0], None)


_ROPE_SWAP = np.concatenate([np.arange(16, 32), np.arange(0, 16), np.arange(48, 64), np.arange(32, 48)])


def _rope_table(seq, rotate):
    if not rotate:
        row = np.concatenate([np.ones(ROPE_D), np.zeros(ROPE_D)]).astype(np.float32)
        return jnp.asarray(np.broadcast_to(row, (seq, 2 * ROPE_D)))
    t = jnp.arange(seq)
    half = ROPE_D // 4
    inv = ROPE_BASE ** (-jnp.arange(half, dtype=F32) / half)
    ang_r = (t // GRID_W).astype(F32)[:, None] * inv[None, :]
    ang_c = (t % GRID_W).astype(F32)[:, None] * inv[None, :]
    cos = jnp.concatenate([jnp.cos(ang_r)] * 2 + [jnp.cos(ang_c)] * 2, axis=-1)
    sin = jnp.concatenate([-jnp.sin(ang_r), jnp.sin(ang_r), -jnp.sin(ang_c), jnp.sin(ang_c)], axis=-1)
    return jnp.concatenate([cos, sin], axis=-1)


def _rotate_rope(r, tab, lane_lo):
    rot = r * tab
    rot = rot + pltpu.roll(rot, ROPE_D, 1)
    return jnp.where(lane_lo, rot, 0.0)


def _qup_kernel(cq_ref, nw_ref, w_ref, gn_ref, gr_ref, tab_ref, o_ref, h_ref):
    @pl.when(pl.program_id(1) == 0)
    def _():
        h_ref[...] = _rms(cq_ref[...].astype(F32), nw_ref[...]).astype(BF16)

    hw = NOPE_D + 2 * ROPE_D
    scale = (NOPE_D + ROPE_D) ** -0.5 * LOG2E
    lane_lo = lax.broadcasted_iota(jnp.int32, (1, LANES), 1) < ROPE_D
    tab = tab_ref[...] * gr_ref[...]
    y = _dot(h_ref[...], w_ref[...])
    ones_all = _ones_rows(LANES)
    ones_lo = _ones_rows(ROPE_D)
    gain_n = gn_ref[...] * scale
    for hh in range(y.shape[1] // hw):
        yn = y[:, hh * hw:hh * hw + NOPE_D]
        yr = y[:, hh * hw + NOPE_D:(hh + 1) * hw]
        o_ref[:, hh * hw:hh * hw + NOPE_D] = _head_rms(yn, gain_n, ones_all).astype(o_ref.dtype)
        ms_r = _dot((yr * yr).astype(BF16), ones_lo) * (1.0 / ROPE_D)
        qr = _rotate_rope(yr, tab, lane_lo) * (lax.rsqrt(ms_r + EPS) * scale)
        o_ref[:, hh * hw + NOPE_D:(hh + 1) * hw] = qr.astype(o_ref.dtype)


def _q_up(proj, col_blk, q_lora, seq, a_norm, w, gain_n, gain_r, table):
    m = proj.shape[0]
    n = w.shape[1]
    tm = _pick_tile(seq, ROW_TILE)
    tn = _pick_tile(n, 4 * COL_TILE)
    nt = table.shape[0] // tm
    return pl.pallas_call(
        _qup_kernel,
        grid=(m // tm, n // tn),
        in_specs=[
            pl.BlockSpec((tm, q_lora), lambda i, j: (i, col_blk)),
            pl.BlockSpec((1, q_lora), lambda i, j: (0, 0)),
            pl.BlockSpec((q_lora, tn), lambda i, j: (0, j)),
            pl.BlockSpec((1, NOPE_D), lambda i, j: (0, 0)),
            pl.BlockSpec((1, 2 * ROPE_D), lambda i, j: (0, 0)),
            pl.BlockSpec((tm, 2 * ROPE_D), lambda i, j: (i % nt, 0)),
        ],
        out_specs=pl.BlockSpec((tm, tn), lambda i, j: (i, j)),
        out_shape=jax.ShapeDtypeStruct((m, n), BF16),
        scratch_shapes=[pltpu.VMEM((tm, q_lora), BF16)],
        compiler_params=_cparams(2),
        name="mla_q_up",
    )(proj, a_norm.reshape(1, q_lora), w, gain_n.reshape(1, NOPE_D), gain_r.reshape(1, 2 * ROPE_D), table)


def _kvup_kernel(*refs, normalize_in, emit_cache, heads):
    refs = list(refs)
    ckv_ref, kr_ref, nw_ref, gkr_ref, w_ref, gk_ref, tab_ref, k_ref, v_ref = refs[:9]
    lane_lo = lax.broadcasted_iota(jnp.int32, (1, LANES), 1) < ROPE_D
    c = ckv_ref[...].astype(F32)
    r = kr_ref[...].astype(F32)
    if normalize_in:
        c = _rms(c, nw_ref[...])
        ms = jnp.sum(jnp.where(lane_lo, r * r, 0.0), axis=-1, keepdims=True) * (1.0 / ROPE_D)
        r = r * lax.rsqrt(ms + EPS) * gkr_ref[...]
    if emit_cache:
        refs[9][...] = c
        refs[10][...] = r[:, :ROPE_D]
    k_rope = _rotate_rope(r, tab_ref[...], lane_lo).astype(k_ref.dtype)
    y = _dot(c.astype(BF16), w_ref[...])
    hw = 2 * LANES
    for h in range(heads):
        k_ref[:, h * hw:h * hw + NOPE_D] = _rms(y[:, h * hw:h * hw + NOPE_D], gk_ref[...]).astype(k_ref.dtype)
        k_ref[:, h * hw + NOPE_D:(h + 1) * hw] = k_rope
        v_ref[:, h * LANES:(h + 1) * LANES] = y[:, h * hw + NOPE_D:(h + 1) * hw].astype(v_ref.dtype)


def _kv_up(ckv_src, ckv_blk, kr_src, kr_blk, seq, kv_lora, heads, a_norm, gain_kr, w, gain_k, table,
           *, normalize_in, emit_cache):
    m = ckv_src.shape[0]
    tm = _pick_tile(seq, ROW_TILE)
    nt = table.shape[0] // tm
    out_specs = [
        pl.BlockSpec((tm, heads * 2 * LANES), lambda i: (i, 0)),
        pl.BlockSpec((tm, heads * LANES), lambda i: (i, 0)),
    ]
    out_shape = [
        jax.ShapeDtypeStruct((m, heads * 2 * LANES), BF16),
        jax.ShapeDtypeStruct((m, heads * LANES), BF16),
    ]
    if emit_cache:
        out_specs += [pl.BlockSpec((tm, kv_lora), lambda i: (i, 0)), pl.BlockSpec((tm, ROPE_D), lambda i: (i, 0))]
        out_shape += [jax.ShapeDtypeStruct((m, kv_lora), F32), jax.ShapeDtypeStruct((m, ROPE_D), F32)]
    return pl.pallas_call(
        functools.partial(_kvup_kernel, normalize_in=normalize_in, emit_cache=emit_cache, heads=heads),
        grid=(m // tm,),
        in_specs=[
            pl.BlockSpec((tm, kv_lora), lambda i: (i, ckv_blk)),
            pl.BlockSpec((tm, 2 * ROPE_D), lambda i: (i, kr_blk)),
            pl.BlockSpec((1, kv_lora), lambda i: (0, 0)),
            pl.BlockSpec((1, 2 * ROPE_D), lambda i: (0, 0)),
            pl.BlockSpec((kv_lora, heads * 2 * LANES), lambda i: (0, 0)),
            pl.BlockSpec((1, NOPE_D), lambda i: (0, 0)),
            pl.BlockSpec((tm, 2 * ROPE_D), lambda i: (i % nt, 0)),
        ],
        out_specs=out_specs,
        out_shape=out_shape,
        compiler_params=_cparams(1),
        name="mla_kv_up",
    )(ckv_src, kr_src, a_norm.reshape(1, kv_lora), gain_kr.reshape(1, 2 * ROPE_D), w,
      gain_k.reshape(1, NOPE_D), table)


def _mla_attn_kernel(*refs, has_ctx):
    if has_ctx:
        q_ref, k_ref, v_ref, kc_ref, vc_ref, g_ref, o_ref, vt_ref, vct_ref = refs
    else:
        q_ref, k_ref, v_ref, g_ref, o_ref, vt_ref = refs
    qw = 2 * LANES
    hp = q_ref.shape[-1] // qw

    @pl.when(pl.program_id(2) == 0)
    def _():
        for hh in range(hp):
            sl = slice(hh * LANES, (hh + 1) * LANES)
            vt_ref[hh] = v_ref[:, sl].astype(F32).T.astype(BF16)
            if has_ctx:
                vct_ref[hh] = vc_ref[:, sl].astype(F32).T.astype(BF16)

    scores = []
    for hh in range(hp):
        q = q_ref[:, hh * qw:(hh + 1) * qw]
        s = _dot_nt(k_ref[:, hh * qw:(hh + 1) * qw], q)
        s_c = _dot_nt(kc_ref[:, hh * qw:(hh + 1) * qw], q) if has_ctx else None
        scores.append((s, s_c))
    for hh, (s, s_c) in enumerate(scores):
        sl = slice(hh * LANES, (hh + 1) * LANES)
        mx = jnp.max(s, axis=0, keepdims=True)
        if has_ctx:
            mx = jnp.maximum(mx, jnp.max(s_c, axis=0, keepdims=True))
        p = jnp.exp2(s - mx)
        l = jnp.sum(p, axis=0, keepdims=True)
        acc = _dot(vt_ref[hh], p.astype(BF16))
        if has_ctx:
            p_c = jnp.exp2(s_c - mx)
            l = l + jnp.sum(p_c, axis=0, keepdims=True)
            acc = acc + _dot(vct_ref[hh], p_c.astype(BF16))
        o = (acc / l).T
        o_ref[:, sl] = (o * _silu(g_ref[:, sl].astype(F32))).astype(o_ref.dtype)


def _mla_attention(q, k, v, proj, gate_blk, batch, seq, heads, hp, ctx=None):
    tq = _pick_tile(seq, ATTN_Q)
    nq = seq // tq
    assert heads % hp == 0 and gate_blk % hp == 0
    qmap = lambda b, h, t: (b * nq + t, h)
    in_specs = [
        pl.BlockSpec((tq, hp * 2 * LANES), qmap),
        pl.BlockSpec((seq, hp * 2 * LANES), lambda b, h, t: (b, h)),
        pl.BlockSpec((seq, hp * LANES), lambda b, h, t: (b, h)),
    ]
    args = [q, k, v]
    if ctx is not None:
        k_c, v_c = ctx
        lc = k_c.shape[0] // batch
        in_specs += [
            pl.BlockSpec((lc, hp * 2 * LANES), lambda b, h, t: (b, h)),
            pl.BlockSpec((lc, hp * LANES), lambda b, h, t: (b, h)),
        ]
        args += [k_c, v_c]
    in_specs.append(pl.BlockSpec((tq, hp * LANES), lambda b, h, t: (b * nq + t, gate_blk // hp + h)))
    args.append(proj)
    scratch = [pltpu.VMEM((hp, LANES, seq), BF16)]
    if ctx is not None:
        scratch.append(pltpu.VMEM((hp, LANES, lc), BF16))
    return pl.pallas_call(
        functools.partial(_mla_attn_kernel, has_ctx=ctx is not None),
        grid=(batch, heads // hp, nq),
        in_specs=in_specs,
        out_specs=pl.BlockSpec((tq, hp * LANES), qmap),
        out_shape=jax.ShapeDtypeStruct((batch * seq, heads * LANES), BF16),
        scratch_shapes=scratch,
        compiler_params=_cparams(3),
        name="mla_attention",
    )(*args)


def _relayout_kernel(w_ref, o_ref, *, plan):
    end = 0
    for dst, src, wd in plan:
        o_ref[:, dst:dst + wd] = w_ref[0, :, src:src + wd].astype(o_ref.dtype)
        end = max(end, dst + wd)
    if end < o_ref.shape[1]:
        o_ref[:, end:] = jnp.zeros((o_ref.shape[0], o_ref.shape[1] - end), o_ref.dtype)


def _relayout_columns(w, j, plan, n_out):
    _, k, n = w.shape
    tr = _pick_tile(k, RELAYOUT_ROWS)
    return pl.pallas_call(
        functools.partial(_relayout_kernel, plan=plan),
        grid=(k // tr,),
        in_specs=[pl.BlockSpec((1, tr, n), lambda i: (j, i, 0))],
        out_specs=pl.BlockSpec((tr, n_out), lambda i: (i, 0)),
        out_shape=jax.ShapeDtypeStruct((k, n_out), BF16),
        compiler_params=_cparams(1),
        name="weight_relayout",
    )(w)


def _pad_cols(w, mult):
    pad = (-w.shape[1]) % mult
    return w if pad == 0 else jnp.concatenate([w, jnp.zeros((w.shape[0], pad), w.dtype)], axis=1)


def kernel(x_prompt, x_sample, cache_na_k, cache_na_v, state_hgrn, cache_mla_ckv, cache_mla_krope, c, c_ctx,
           norm_w, w_ada, b_ada, w_out,
           w_in_ab, na_q_norm, na_k_norm, na_rpb, sg_norm, sg_w, sg_b,
           w_in_cd, hgrn_lb, hgrn_out_norm, mla_q_a_norm, mla_w_q_up, mla_kv_a_norm, mla_w_kv_up,
           mla_q_norm, mla_k_norm):
    bp, lp, d = x_prompt.shape
    bs, ls, _ = x_sample.shape
    depth = norm_w.shape[0]
    hd = cache_na_k.shape[-1]
    heads = cache_na_k.shape[-2]
    width = heads * hd
    assert hd == LANES and state_hgrn.shape[-2:] == (hd, hd) and ls % GRID_W == 0
    q_lora = mla_q_a_norm.shape[-1]
    kv_lora = mla_kv_a_norm.shape[-1]
    assert mla_q_norm.shape[-1] == NOPE_D + ROPE_D and cache_mla_krope.shape[-1] == ROPE_D
    assert mla_w_kv_up.shape[-1] == heads * 2 * LANES

    n_cond = bs + 1
    cond_rows = -(-n_cond // 8) * 8
    cond = jnp.concatenate([c, c_ctx[None, :], jnp.zeros((cond_rows - n_cond, d), F32)], axis=0)
    mod = _modulation(cond, w_ada, b_ada).reshape(depth, cond_rows, 3, 1, d)

    lb_soft = jax.nn.softmax(hgrn_lb.astype(F32), axis=0)
    lb_cum = jnp.cumsum(lb_soft, axis=0)
    lower_bounds = lb_cum - lb_cum[:1]

    xp = x_prompt.reshape(bp * lp, d)
    xs = x_sample.reshape(bs * ls, d)
    na_k_new, na_v_new, hgrn_new, ckv_new, kr_new = [], [], [], [], []
    for layer in range(depth):
        j = layer // 2
        shift_s, scale_s, gate_s = (mod[layer, :bs, i] for i in range(3))
        shift_p, scale_p, gate_p = (mod[layer, bs:bs + 1, i] for i in range(3))
        w_o = w_out[layer].astype(BF16)
        if layer % 2 == 0:
            w_in = _pad_cols(w_in_ab[j].astype(BF16), COL_TILE)
            proj_p = _in_proj(xp, lp, norm_w[layer], scale_p, shift_p, w_in)
            proj_s = _in_proj(xs, ls, norm_w[layer], scale_s, shift_s, w_in)
            mix_a_p, k_new, v_new = _attn_ctx0(proj_p, bp, lp, heads, na_q_norm[j], na_k_norm[j])
            na_k_new.append(k_new.reshape(bp, lp, heads, hd))
            na_v_new.append(v_new.reshape(bp, lp, heads, hd))
            bias = _na_bias_tables(na_rpb[j], ls // GRID_W)
            lc = cache_na_k.shape[2]
            mix_a_s = _na_latent(proj_s, bs, ls, heads,
                                 cache_na_k[:, j].reshape(bs, lc, width).astype(BF16),
                                 cache_na_v[:, j].reshape(bs, lc, width).astype(BF16),
                                 bias, na_q_norm[j], na_k_norm[j])
            mix_b_p = _spatial_gating(proj_p, width, sg_norm[j], sg_w[j], sg_b[j])
            mix_b_s = _spatial_gating(proj_s, width, sg_norm[j], sg_w[j], sg_b[j])
        else:
            kr_src = 5 * width + q_lora + kv_lora
            gd_src = kr_src + ROPE_D
            kr_dst = 6 * width + q_lora + kv_lora
            half = ROPE_D // 4
            plan = [(0, 0, 5 * width), (5 * width, gd_src, width), (6 * width, 5 * width, q_lora + kv_lora),
                    (kr_dst, kr_src, ROPE_D)]
            plan += [(kr_dst + ROPE_D + i * half, kr_src + int(_ROPE_SWAP[i * half]), half) for i in range(4)]
            w_in = _relayout_columns(w_in_cd, j, plan, -(-(kr_dst + 2 * ROPE_D) // COL_TILE) * COL_TILE)
            assert (6 * width) % q_lora == 0 and (6 * width + q_lora) % kv_lora == 0
            cq_blk = (6 * width) // q_lora
            ckv_blk = (6 * width + q_lora) // kv_lora
            kr_blk = (6 * width + q_lora + kv_lora) // LANES
            gd_blk = (5 * width) // LANES
            proj_p = _in_proj(xp, lp, norm_w[layer], scale_p, shift_p, w_in)
            proj_s = _in_proj(xs, ls, norm_w[layer], scale_s, shift_s, w_in)

            lb_f, lb_b = lower_bounds[layer, 0], lower_bounds[layer, 1]
            of_p, st_f = _hgrn_pass(proj_p, bp, lp, heads, lb_f, None, rev=False, emit_state=True)
            mix_a_p, st_b = _hgrn_pass(proj_p, bp, lp, heads, lb_b, None, rev=True, emit_state=True,
                                       prev=of_p, out_norm=hgrn_out_norm[j])
            hgrn_new.append(jnp.stack([st_f, st_b], axis=1))
            of_s, _ = _hgrn_pass(proj_s, bs, ls, heads, lb_f, state_hgrn[:, j, 0], rev=False, emit_state=False)
            mix_a_s, _ = _hgrn_pass(proj_s, bs, ls, heads, lb_b, state_hgrn[:, j, 1], rev=True, emit_state=False,
                                    prev=of_s, out_norm=hgrn_out_norm[j])

            wq = mla_w_q_up[j].reshape(q_lora, heads, NOPE_D + ROPE_D)
            wq = jnp.concatenate([wq, wq[:, :, NOPE_D:][:, :, _ROPE_SWAP]], axis=-1)
            wq = wq.reshape(q_lora, heads * 2 * LANES).astype(BF16)
            wkv = mla_w_kv_up[j].astype(BF16)
            gq, gk = mla_q_norm[j], mla_k_norm[j]
            gq_r = jnp.concatenate([gq[NOPE_D:], gq[NOPE_D:][_ROPE_SWAP]])
            gk_r = jnp.concatenate([gk[NOPE_D:], gk[NOPE_D:][_ROPE_SWAP]])
            tab_p = _rope_table(_pick_tile(lp, ROW_TILE), False)
            tab_s = _rope_table(ls, True)
            tab_c = _rope_table(_pick_tile(cache_mla_ckv.shape[2], ROW_TILE), False)

            q_p = _q_up(proj_p, cq_blk, q_lora, lp, mla_q_a_norm[j], wq, gq[:NOPE_D], gq_r, tab_p)
            k_p, v_p, ckv_n, kr_n = _kv_up(proj_p, ckv_blk, proj_p, kr_blk, lp, kv_lora, heads, mla_kv_a_norm[j],
                                           gk_r, wkv, gk[:NOPE_D], tab_p, normalize_in=True, emit_cache=True)
            ckv_new.append(ckv_n.reshape(bp, lp, kv_lora))
            kr_new.append(kr_n.reshape(bp, lp, ROPE_D))
            mix_b_p = _mla_attention(q_p, k_p, v_p, proj_p, gd_blk, bp, lp, heads, CTX_ATTN_HEADS)

            q_s = _q_up(proj_s, cq_blk, q_lora, ls, mla_q_a_norm[j], wq, gq[:NOPE_D], gq_r, tab_s)
            k_s, v_s = _kv_up(proj_s, ckv_blk, proj_s, kr_blk, ls, kv_lora, heads, mla_kv_a_norm[j],
                              gk_r, wkv, gk[:NOPE_D], tab_s, normalize_in=True, emit_cache=False)
            lc = cache_mla_ckv.shape[2]
            kr_c = cache_mla_krope[:, j].reshape(bs * lc, ROPE_D)
            kr_c = jnp.concatenate([kr_c, jnp.zeros_like(kr_c)], axis=1)
            k_c, v_c = _kv_up(cache_mla_ckv[:, j].reshape(bs * lc, kv_lora), 0, kr_c, 0, lc, kv_lora, heads,
                              mla_kv_a_norm[j], gk_r, wkv, gk[:NOPE_D], tab_c, normalize_in=False, emit_cache=False)
            mix_b_s = _mla_attention(q_s, k_s, v_s, proj_s, gd_blk, bs, ls, heads, LAT_ATTN_HEADS, ctx=(k_c, v_c))
        xp = _out_proj(mix_a_p, mix_b_p, w_o, xp, lp, gate_p)
        xs = _out_proj(mix_a_s, mix_b_s, w_o, xs, ls, gate_s)
    return (xp.reshape(bp, lp, d), xs.reshape(bs, ls, d),
            jnp.stack(na_k_new, axis=1), jnp.stack(na_v_new, axis=1), jnp.stack(hgrn_new, axis=1),
            jnp.stack(ckv_new, axis=1), jnp.stack(kr_new, axis=1))
```
